```python
import math
import jax
import jax.numpy as jnp
from jax import lax
import numpy as np

D_MODEL = 2048
BATCH = 4
SEQ = 2048
DEPTH = 1

MIX_WIDTH = D_MODEL
GDN_HEADS = 8
GDN_HEAD_DIM = 128
GDN_WIDTH = GDN_HEADS * GDN_HEAD_DIM
GDN_CONV = 4
GDN_CHUNK = 64
MOBA_HEADS = 8
MOBA_HEAD_DIM = (MIX_WIDTH - GDN_WIDTH) // MOBA_HEADS
MOBA_WIDTH = MOBA_HEADS * MOBA_HEAD_DIM
MOBA_BLOCK = 256
MOBA_TOPK = 3
MOBA_Q_CHUNK = 32
ROPE_DIM = MOBA_HEAD_DIM // 4
ROPE_THETA = 500000.0
D_FF = 256 * ((8 * D_MODEL // 3 + 255) // 256)
N_MOD = 9
NORM_EPS = 1e-6
IN_COLS = 4 * GDN_WIDTH + 2 * GDN_HEADS + 3 * MOBA_WIDTH

kernel_name = "hymba_gdn_moba_macaron_adaln_block"


def rms_norm(x, g):
    xf = x.astype(jnp.float32)
    y = xf * lax.rsqrt(jnp.mean(xf * xf, axis=-1, keepdims=True) + NORM_EPS)
    return (y * g.astype(jnp.float32)).astype(x.dtype)


def l2_normalize(x):
    return x * lax.rsqrt(jnp.sum(x * x, axis=-1, keepdims=True) + NORM_EPS)


def swiglu(h, w_gate, w_up, w_down):
    return (jax.nn.silu(h @ w_gate) * (h @ w_up)) @ w_down


def causal_depthwise_conv(x, w):
    K = w.shape[0]
    S = x.shape[1]
    xp = jnp.pad(x, ((0, 0), (K - 1, 0), (0, 0)))
    y = xp[:, 0:S] * w[0]
    for j in range(1, K):
        y = y + xp[:, j:j + S] * w[j]
    return y


def partial_rope(x, pos):
    inv_freq = ROPE_THETA ** (-jnp.arange(0, ROPE_DIM, 2, dtype=jnp.float32) / ROPE_DIM)
    ang = pos.astype(jnp.float32)[:, None] * inv_freq[None, :]
    cos, sin = jnp.cos(ang), jnp.sin(ang)
    x_rot, x_pass = x[..., :ROPE_DIM], x[..., ROPE_DIM:]
    half = ROPE_DIM // 2
    x1 = x_rot[..., :half].astype(jnp.float32)
    x2 = x_rot[..., half:].astype(jnp.float32)
    rot = jnp.concatenate([x1 * cos - x2 * sin, x2 * cos + x1 * sin], axis=-1)
    return jnp.concatenate([rot.astype(x.dtype), x_pass], axis=-1)


def chunked_gated_delta_rule(q, k, v, g, beta):
    B, H, S, dk = q.shape
    dv = v.shape[-1]
    C = GDN_CHUNK
    S_pad = -(-S // C) * C
    pad = S_pad - S
    q, k, v = [jnp.pad(t, ((0, 0), (0, 0), (0, pad), (0, 0))) for t in (q, k, v)]
    g, beta = [jnp.pad(t, ((0, 0), (0, 0), (0, pad))) for t in (g, beta)]
    N = S_pad // C
    q = q * (dk ** -0.5)
    qc = q.reshape(B, H, N, C, dk)
    kc = k.reshape(B, H, N, C, dk)
    vc = v.reshape(B, H, N, C, dv)
    bc = beta.reshape(B, H, N, C)
    G = jnp.cumsum(g.reshape(B, H, N, C), axis=-1)
    tril = jnp.tril(jnp.ones((C, C), dtype=bool))
    strict = jnp.tril(jnp.ones((C, C), dtype=bool), -1)
    decay = jnp.exp(jnp.where(tril, G[..., :, None] - G[..., None, :], -jnp.inf))
    k_beta = kc * bc[..., None]
    v_beta = vc * bc[..., None]
    L = jnp.where(strict, jnp.einsum('bhnid,bhnjd->bhnij', k_beta, kc) * decay, 0.0)
    u = lax.linalg.triangular_solve(L, v_beta, left_side=True, lower=True, unit_diagonal=True)
    w = lax.linalg.triangular_solve(L, k_beta * jnp.exp(G)[..., None], left_side=True, lower=True,
                                    unit_diagonal=True)
    attn_intra = jnp.where(tril, jnp.einsum('bhnid,bhnjd->bhnij', qc, kc) * decay, 0.0)

    def step(state, inp):
        q_i, k_i, u_i, w_i, G_i, A_i = inp
        v_new = u_i - jnp.einsum('bhck,bhkv->bhcv', w_i, state)
        o_i = (jnp.einsum('bhck,bhkv->bhcv', q_i * jnp.exp(G_i)[..., None], state)
               + jnp.einsum('bhcj,bhjv->bhcv', A_i, v_new))
        g_last = G_i[..., -1]
        k_dec = k_i * jnp.exp(g_last[..., None] - G_i)[..., None]
        state = state * jnp.exp(g_last)[..., None, None] + jnp.einsum('bhck,bhcv->bhkv', k_dec, v_new)
        return state, o_i

    xs = tuple(jnp.moveaxis(t, 2, 0) for t in (qc, kc, u, w, G, attn_intra))
    state0 = jnp.zeros((B, H, dk, dv), jnp.float32)
    _, o = lax.scan(step, state0, xs)
    o = jnp.moveaxis(o, 0, 2).reshape(B, H, S_pad, dv)
    return o[:, :, :S]


def gated_deltanet(q, k, v, z, a, b, conv_w, a_log, dt_bias, norm_g):
    B, S, _ = q.shape
    dtype = q.dtype
    qkv = jax.nn.silu(causal_depthwise_conv(jnp.concatenate([q, k, v], axis=-1), conv_w))
    q, k, v = jnp.split(qkv.astype(jnp.float32), 3, axis=-1)
    to_heads = lambda t: t.reshape(B, S, GDN_HEADS, GDN_HEAD_DIM).transpose(0, 2, 1, 3)
    q = l2_normalize(to_heads(q))
    k = l2_normalize(to_heads(k))
    v = to_heads(v)
    beta = jax.nn.sigmoid(b.astype(jnp.float32)).transpose(0, 2, 1)
    g = (-jnp.exp(a_log.astype(jnp.float32))
         * jax.nn.softplus(a.astype(jnp.float32) + dt_bias.astype(jnp.float32))).transpose(0, 2, 1)
    o = chunked_gated_delta_rule(q, k, v, g, beta).transpose(0, 2, 1, 3)
    zf = z.astype(jnp.float32).reshape(B, S, GDN_HEADS, GDN_HEAD_DIM)
    o = rms_norm(o, norm_g) * jax.nn.silu(zf)
    return o.reshape(B, S, GDN_WIDTH).astype(dtype)


def moba_attention(q, k, v):
    B, H, S, hd = q.shape
    S_pad = -(-S // MOBA_BLOCK) * MOBA_BLOCK
    pad = S_pad - S
    q, k, v = [jnp.pad(t, ((0, 0), (0, 0), (0, pad), (0, 0))) for t in (q, k, v)]
    NB = S_pad // MOBA_BLOCK
    top = min(MOBA_TOPK, NB)
    kb = k.reshape(B, H, NB, MOBA_BLOCK, hd)
    vb = v.reshape(B, H, NB, MOBA_BLOCK, hd)
    kmean = jnp.mean(kb.astype(jnp.float32), axis=3)
    NQ = S_pad // MOBA_Q_CHUNK
    q_chunks = jnp.moveaxis(q.reshape(B, H, NQ, MOBA_Q_CHUNK, hd), 2, 0)
    bi = jnp.arange(B)[:, None, None, None]
    hi = jnp.arange(H)[None, :, None, None]
    scale = hd ** -0.5

    def one_chunk(args):
        q_i, ci = args
        qpos = ci * MOBA_Q_CHUNK + jnp.arange(MOBA_Q_CHUNK)
        own = (ci * MOBA_Q_CHUNK) // MOBA_BLOCK
        gate = jnp.einsum('bhqd,bhnd->bhqn', q_i.astype(jnp.float32), kmean)
        gate = jnp.where(jnp.arange(NB) < own, gate, -jnp.inf)
        _, sel = lax.top_k(gate, top)
        valid = jnp.arange(top) < own
        k_sel = kb[bi, hi, sel]
        v_sel = vb[bi, hi, sel]
        k_own = lax.dynamic_index_in_dim(kb, own, axis=2, keepdims=False)
        v_own = lax.dynamic_index_in_dim(vb, own, axis=2, keepdims=False)
        s_sel = jnp.einsum('bhqd,bhqtkd->bhqtk', q_i, k_sel).astype(jnp.float32) * scale
        s_sel = jnp.where(valid[:, None], s_sel, -jnp.inf)
        kpos = own * MOBA_BLOCK + jnp.arange(MOBA_BLOCK)
        s_own = jnp.einsum('bhqd,bhkd->bhqk', q_i, k_own).astype(jnp.float32) * scale
        s_own = jnp.where(kpos[None, :] <= qpos[:, None], s_own, -jnp.inf)
        s = jnp.concatenate([s_sel.reshape(B, H, MOBA_Q_CHUNK, top * MOBA_BLOCK), s_own], axis=-1)
        p = jax.nn.softmax(s, axis=-1).astype(v.dtype)
        p_sel = p[..., :top * MOBA_BLOCK].reshape(B, H, MOBA_Q_CHUNK, top, MOBA_BLOCK)
        p_own = p[..., top * MOBA_BLOCK:]
        return (jnp.einsum('bhqtk,bhqtkd->bhqd', p_sel, v_sel)
                + jnp.einsum('bhqk,bhkd->bhqd', p_own, v_own))

    out = lax.map(one_chunk, (q_chunks, jnp.arange(NQ)))
    out = jnp.moveaxis(out, 0, 2).reshape(B, H, S_pad, hd)
    return out[:, :, :S]


def hybrid_mixer(h, w_in, w_out, conv_w, a_log, dt_bias, norm_g, pos):
    B, S, _ = h.shape
    proj = h @ w_in
    sizes = [GDN_WIDTH] * 4 + [GDN_HEADS] * 2 + [MOBA_WIDTH] * 3
    cuts = [int(v) for v in np.cumsum(sizes)[:-1]]
    gq, gk, gv, gz, ga, gb, mq, mk, mv = jnp.split(proj, cuts, axis=-1)
    y_gdn = gated_deltanet(gq, gk, gv, gz, ga, gb, conv_w, a_log, dt_bias, norm_g)
    to_heads = lambda t: t.reshape(B, S, MOBA_HEADS, MOBA_HEAD_DIM).transpose(0, 2, 1, 3)
    mq = partial_rope(to_heads(mq), pos)
    mk = partial_rope(to_heads(mk), pos)
    y_moba = moba_attention(mq, mk, to_heads(mv)).transpose(0, 2, 1, 3).reshape(B, S, MOBA_WIDTH)
    return jnp.concatenate([y_gdn, y_moba.astype(y_gdn.dtype)], axis=-1) @ w_out


def setup_inputs(seed: int = 0) -> dict:
    key = jax.random.key(seed)
    ks = jax.random.split(key, 24)
    f32 = jnp.float32

    def dense(k, fan_in, shape, gain=1.0):
        return jax.random.normal(k, shape, f32) * (gain * fan_in ** -0.5)

    def gain_vec(k, n):
        return 1.0 + 0.1 * jax.random.normal(k, (DEPTH, n), f32)

    x = jax.random.normal(ks[0], (BATCH, SEQ, D_MODEL), f32)
    c = jax.random.normal(ks[1], (BATCH, D_MODEL), f32)
    w_ada = dense(ks[2], D_MODEL, (DEPTH, D_MODEL, N_MOD * D_MODEL), 0.5)
    b_ada = 0.02 * jax.random.normal(ks[3], (DEPTH, N_MOD * D_MODEL), f32)
    ffn1_pre_g = gain_vec(ks[4], D_MODEL)
    ffn1_post_g = gain_vec(ks[5], D_MODEL)
    ffn1_w_gate = dense(ks[6], D_MODEL, (DEPTH, D_MODEL, D_FF))
    ffn1_w_up = dense(ks[7], D_MODEL, (DEPTH, D_MODEL, D_FF))
    ffn1_w_down = dense(ks[8], D_FF, (DEPTH, D_FF, D_MODEL))
    mix_pre_g = gain_vec(ks[9], D_MODEL)
    mix_post_g = gain_vec(ks[10], D_MODEL)
    w_in = dense(ks[11], D_MODEL, (DEPTH, D_MODEL, IN_COLS))
    gdn_conv_w = dense(ks[12], GDN_CONV, (DEPTH, GDN_CONV, 3 * GDN_WIDTH))
    gdn_a_log = jnp.log(jax.random.uniform(ks[13], (DEPTH, GDN_HEADS), f32, 1.0, 16.0))
    dt = jnp.exp(jax.random.uniform(ks[14], (DEPTH, GDN_HEADS), f32, math.log(1e-3), math.log(1e-1)))
    gdn_dt_bias = dt + jnp.log(-jnp.expm1(-dt))
    gdn_norm_g = gain_vec(ks[15], GDN_HEAD_DIM)
    w_out = dense(ks[16], MIX_WIDTH, (DEPTH, MIX_WIDTH, D_MODEL))
    ffn2_pre_g = gain_vec(ks[17], D_MODEL)
    ffn2_post_g = gain_vec(ks[18], D_MODEL)
    ffn2_w_gate = dense(ks[19], D_MODEL, (DEPTH, D_MODEL, D_FF))
    ffn2_w_up = dense(ks[20], D_MODEL, (DEPTH, D_MODEL, D_FF))
    ffn2_w_down = dense(ks[21], D_FF, (DEPTH, D_FF, D_MODEL))
    return {"x": x, "c": c, "w_ada": w_ada, "b_ada": b_ada,
            "ffn1_pre_g": ffn1_pre_g, "ffn1_post_g": ffn1_post_g,
            "ffn1_w_gate": ffn1_w_gate, "ffn1_w_up": ffn1_w_up, "ffn1_w_down": ffn1_w_down,
            "mix_pre_g": mix_pre_g, "mix_post_g": mix_post_g, "w_in": w_in,
            "gdn_conv_w": gdn_conv_w, "gdn_a_log": gdn_a_log, "gdn_dt_bias": gdn_dt_bias,
            "gdn_norm_g": gdn_norm_g, "w_out": w_out,
            "ffn2_pre_g": ffn2_pre_g, "ffn2_post_g": ffn2_post_g,
            "ffn2_w_gate": ffn2_w_gate, "ffn2_w_up": ffn2_w_up, "ffn2_w_down": ffn2_w_down}


def reference(x, c, w_ada, b_ada, ffn1_pre_g, ffn1_post_g, ffn1_w_gate, ffn1_w_up, ffn1_w_down,
              mix_pre_g, mix_post_g, w_in, gdn_conv_w, gdn_a_log, gdn_dt_bias, gdn_norm_g, w_out,
              ffn2_pre_g, ffn2_post_g, ffn2_w_gate, ffn2_w_up, ffn2_w_down):
    B, S, _ = x.shape
    pos = jnp.arange(S)
    for l in range(DEPTH):
        mod = (jax.nn.silu(c) @ w_ada[l] + b_ada[l]).reshape(B, N_MOD, D_MODEL)[:, :, None, :]
        sh1, sc1, ga1, sh2, sc2, ga2, sh3, sc3, ga3 = [mod[:, i] for i in range(N_MOD)]
        h = rms_norm(x, ffn1_pre_g[l]) * (1 + sc1) + sh1
        y = swiglu(h, ffn1_w_gate[l], ffn1_w_up[l], ffn1_w_down[l])
        x = x + 0.5 * ga1 * rms_norm(y, ffn1_post_g[l])
        h = rms_norm(x, mix_pre_g[l]) * (1 + sc2) + sh2
        y = hybrid_mixer(h, w_in[l], w_out[l], gdn_conv_w[l], gdn_a_log[l], gdn_dt_bias[l],
                         gdn_norm_g[l], pos)
        x = x + ga2 * rms_norm(y, mix_post_g[l])
        h = rms_norm(x, ffn2_pre_g[l]) * (1 + sc3) + sh3
        y = swiglu(h, ffn2_w_gate[l], ffn2_w_up[l], ffn2_w_down[l])
        x = x + 0.5 * ga3 * rms_norm(y, ffn2_post_g[l])
    return x
```

```python
import functools

import jax
import jax.numpy as jnp
from jax import lax
from jax.experimental import pallas as pl
from jax.experimental.pallas import tpu as pltpu

F32 = jnp.float32
BF16 = jnp.bfloat16

NORM_EPS = 1e-6
N_MOD = 9
GDN_HEADS = 8
GDN_HEAD_DIM = 128
GDN_CONV = 4
GDN_CHUNK = 64
MOBA_HEADS = 8
MOBA_HEAD_DIM = 128
MOBA_BLOCK = 256
MOBA_TOPK = 3
ROPE_DIM = MOBA_HEAD_DIM // 4
ROPE_THETA = 500000.0

LANES = 128
SUBLANES = 8
VMEM_LIMIT = 56 * 1024 * 1024
MASKED = -1e30


def _dot(a, b):
    return jnp.dot(a, b, preferred_element_type=F32)


def _dot_nt(a, b):
    return lax.dot_general(a, b, (((1,), (1,)), ((), ())), preferred_element_type=F32)


def _dot_tn(a, b):
    return lax.dot_general(a, b, (((0,), (0,)), ((), ())), preferred_element_type=F32)


def _rms(x):
    return x * lax.rsqrt(jnp.mean(x * x, axis=-1, keepdims=True) + NORM_EPS)


def _silu(x):
    return x * jax.nn.sigmoid(x)


def _params(*sem):
    return pltpu.CompilerParams(dimension_semantics=sem, vmem_limit_bytes=VMEM_LIMIT)


def _adaln_kernel(c_ref, w_ref, b_ref, o_ref):
    o_ref[...] = _dot(_silu(c_ref[...]), w_ref[...]) + b_ref[...]


def _adaln(c, w, b, *, tn=1024):
    bsz, d = c.shape
    n = w.shape[1]
    rows = -(-bsz // SUBLANES) * SUBLANES
    c_pad = jnp.pad(c, ((0, rows - bsz), (0, 0)))
    out = pl.pallas_call(
        _adaln_kernel,
        out_shape=jax.ShapeDtypeStruct((rows, n), F32),
        grid=(n // tn,),
        in_specs=[pl.BlockSpec((rows, d), lambda j: (0, 0)),
                  pl.BlockSpec((d, tn), lambda j: (0, j)),
                  pl.BlockSpec((1, tn), lambda j: (0, j))],
        out_specs=pl.BlockSpec((rows, tn), lambda j: (0, j)),
        compiler_params=_params("parallel"),
        name="adaln",
    )(c_pad, w, b.reshape(1, n))
    return out[:bsz]


def _mod_norm(x_ref, mod_ref, g_ref, row):
    sh = mod_ref[0, row:row + 1, :]
    sc = mod_ref[0, row + 1:row + 2, :]
    return (_rms(x_ref[...]) * g_ref[...]) * (1.0 + sc) + sh


def _ffn_kernel(x_ref, mod_ref, pre_ref, post_ref, wg_ref, wu_ref, wd_ref, o_ref, h_ref, acc_ref,
                *, row):
    j = pl.program_id(1)

    @pl.when(j == 0)
    def _():
        h_ref[...] = _mod_norm(x_ref, mod_ref, pre_ref, row).astype(BF16)
        acc_ref[...] = jnp.zeros_like(acc_ref)

    h = h_ref[...]
    gate = _dot(h, wg_ref[...])
    up = _dot(h, wu_ref[...])
    act = (_silu(gate) * up).astype(BF16)
    acc_ref[...] += _dot(act, wd_ref[...])

    @pl.when(j == pl.num_programs(1) - 1)
    def _():
        ga = mod_ref[0, row + 2:row + 3, :]
        o_ref[...] = x_ref[...] + 0.5 * ga * (_rms(acc_ref[...]) * post_ref[...])


def _ffn(x2, mod, row, pre_g, post_g, wg, wu, wd, *, seq, tm=512, tf=512):
    m, d = x2.shape
    dff = wg.shape[1]
    per_b = seq // tm
    return pl.pallas_call(
        functools.partial(_ffn_kernel, row=row),
        out_shape=jax.ShapeDtypeStruct((m, d), F32),
        grid=(m // tm, dff // tf),
        in_specs=[pl.BlockSpec((tm, d), lambda i, j: (i, 0)),
                  pl.BlockSpec((1, N_MOD, d), lambda i, j: (i // per_b, 0, 0)),
                  pl.BlockSpec((1, d), lambda i, j: (0, 0)),
                  pl.BlockSpec((1, d), lambda i, j: (0, 0)),
                  pl.BlockSpec((d, tf), lambda i, j: (0, j)),
                  pl.BlockSpec((d, tf), lambda i, j: (0, j)),
                  pl.BlockSpec((tf, d), lambda i, j: (j, 0))],
        out_specs=pl.BlockSpec((tm, d), lambda i, j: (i, 0)),
        scratch_shapes=[pltpu.VMEM((tm, d), BF16), pltpu.VMEM((tm, d), F32)],
        compiler_params=_params("parallel", "arbitrary"),
        name="swiglu_ffn",
    )(x2, mod, pre_g.reshape(1, d), post_g.reshape(1, d), wg, wu, wd)


def _proj_gdn_kernel(x_ref, mod_ref, g_ref, w_ref, o_ref, h_ref, *, row):
    @pl.when(pl.program_id(1) == 0)
    def _():
        h_ref[...] = _mod_norm(x_ref, mod_ref, g_ref, row).astype(BF16)

    o_ref[...] = _dot(h_ref[...], w_ref[...])


def _proj_qk_kernel(x_ref, mod_ref, g_ref, w_ref, cos_ref, sa_ref, sb_ref, o_ref, h_ref, *, row):
    @pl.when(pl.program_id(1) == 0)
    def _():
        h_ref[...] = _mod_norm(x_ref, mod_ref, g_ref, row).astype(BF16)

    y = _dot(h_ref[...], w_ref[...])
    cos, sa, sb = cos_ref[...], sa_ref[...], sb_ref[...]
    half = ROPE_DIM // 2
    for hd in range(y.shape[1] // LANES):
        yh = y[:, hd * LANES:(hd + 1) * LANES]
        rot = (yh * cos + pltpu.roll(yh, LANES - half, 1) * sa + pltpu.roll(yh, half, 1) * sb)
        o_ref[:, hd * LANES:(hd + 1) * LANES] = rot.astype(o_ref.dtype)


def _proj_v_kernel(x_ref, mod_ref, g_ref, wv_ref, wab_ref, v_ref, ab_ref, *, row):
    h = _mod_norm(x_ref, mod_ref, g_ref, row).astype(BF16)
    v_ref[...] = _dot(h, wv_ref[...]).astype(v_ref.dtype)
    ab_ref[...] = _dot(h, wab_ref[...])


def _rope_tables(seq):
    half = ROPE_DIM // 2
    inv_freq = ROPE_THETA ** (-jnp.arange(0, ROPE_DIM, 2, dtype=F32) / ROPE_DIM)
    ang = jnp.arange(seq).astype(F32)[:, None] * inv_freq[None, :]
    cos, sin = jnp.cos(ang), jnp.sin(ang)
    rest = LANES - 2 * half
    zeros = jnp.zeros((seq, half), F32)
    tail0 = jnp.zeros((seq, rest), F32)
    cos_t = jnp.concatenate([cos, cos, jnp.ones((seq, rest), F32)], axis=1)
    sa_t = jnp.concatenate([-sin, zeros, tail0], axis=1)
    sb_t = jnp.concatenate([zeros, sin, tail0], axis=1)
    return cos_t, sa_t, sb_t


def _in_proj(x2, mod, row, g, w_gdn, w_qk, w_v, w_ab, *, seq, tm=512, tn=1024):
    m, d = x2.shape
    per_b = seq // tm
    x_spec2 = pl.BlockSpec((tm, d), lambda i, j: (i, 0))
    mod_spec2 = pl.BlockSpec((1, N_MOD, d), lambda i, j: (i // per_b, 0, 0))
    g_spec2 = pl.BlockSpec((1, d), lambda i, j: (0, 0))
    g2 = g.reshape(1, d)

    n_gdn = w_gdn.shape[1]
    p_gdn = pl.pallas_call(
        functools.partial(_proj_gdn_kernel, row=row),
        out_shape=jax.ShapeDtypeStruct((m, n_gdn), F32),
        grid=(m // tm, n_gdn // tn),
        in_specs=[x_spec2, mod_spec2, g_spec2, pl.BlockSpec((d, tn), lambda i, j: (0, j))],
        out_specs=pl.BlockSpec((tm, tn), lambda i, j: (i, j)),
        scratch_shapes=[pltpu.VMEM((tm, d), BF16)],
        compiler_params=_params("parallel", "arbitrary"),
        name="proj_gdn",
    )(x2, mod, g2, w_gdn)

    n_qk = w_qk.shape[1]
    cos_t, sa_t, sb_t = _rope_tables(seq)
    tab_spec = pl.BlockSpec((tm, LANES), lambda i, j: (i % per_b, 0))
    p_qk = pl.pallas_call(
        functools.partial(_proj_qk_kernel, row=row),
        out_shape=jax.ShapeDtypeStruct((m, n_qk), BF16),
        grid=(m // tm, n_qk // tn),
        in_specs=[x_spec2, mod_spec2, g_spec2, pl.BlockSpec((d, tn), lambda i, j: (0, j)),
                  tab_spec, tab_spec, tab_spec],
        out_specs=pl.BlockSpec((tm, tn), lambda i, j: (i, j)),
        scratch_shapes=[pltpu.VMEM((tm, d), BF16)],
        compiler_params=_params("parallel", "arbitrary"),
        name="proj_moba_qk",
    )(x2, mod, g2, w_qk, cos_t, sa_t, sb_t)

    n_v, n_ab = w_v.shape[1], w_ab.shape[1]
    p_v, p_ab = pl.pallas_call(
        functools.partial(_proj_v_kernel, row=row),
        out_shape=(jax.ShapeDtypeStruct((m, n_v), BF16), jax.ShapeDtypeStruct((m, n_ab), F32)),
        grid=(m // tm,),
        in_specs=[pl.BlockSpec((tm, d), lambda i: (i, 0)),
                  pl.BlockSpec((1, N_MOD, d), lambda i: (i // per_b, 0, 0)),
                  pl.BlockSpec((1, d), lambda i: (0, 0)),
                  pl.BlockSpec((d, n_v), lambda i: (0, 0)),
                  pl.BlockSpec((d, n_ab), lambda i: (0, 0))],
        out_specs=(pl.BlockSpec((tm, n_v), lambda i: (i, 0)),
                   pl.BlockSpec((tm, n_ab), lambda i: (i, 0))),
        compiler_params=_params("parallel"),
        name="proj_moba_v_gates",
    )(x2, mod, g2, w_v, w_ab)
    return p_gdn, p_qk, p_v, p_ab


GDN_GROUP = 4
GDN_HALO = SUBLANES


def _unit_lower_inverse(low, eye):
    inv = eye - low
    power = low
    steps = GDN_CHUNK.bit_length() - 2
    for _ in range(steps):
        pb = power.astype(BF16)
        power = _dot(pb, pb)
        inv = inv + _dot(inv.astype(BF16), power.astype(BF16))
    return inv


def _gdn_kernel(q_ref, k_ref, v_ref, z_ref, ab_ref, cwq_ref, cwk_ref, cwv_ref, alog_ref, dtb_ref,
                ng_ref, o_ref, xq, xk, xv, state_ref, oacc, *, tt):
    t = pl.program_id(2)
    c = GDN_CHUNK
    hd = GDN_HEAD_DIM
    halo = GDN_HALO

    @pl.when(t == 0)
    def _():
        for buf in (xq, xk, xv):
            buf[0:halo, :] = jnp.zeros((halo, buf.shape[1]), F32)
        state_ref[...] = jnp.zeros_like(state_ref)

    @pl.when(t > 0)
    def _():
        for buf in (xq, xk, xv):
            buf[0:halo, :] = buf[tt:tt + halo, :]

    xq[halo:halo + tt, :] = q_ref[...]
    xk[halo:halo + tt, :] = k_ref[...]
    xv[halo:halo + tt, :] = v_ref[...]

    def conv_silu(buf, cw_ref):
        base = halo - (GDN_CONV - 1)
        acc = buf[pl.ds(base, tt), :] * cw_ref[0:1, :]
        for j in range(1, GDN_CONV):
            acc = acc + buf[pl.ds(base + j, tt), :] * cw_ref[j:j + 1, :]
        return _silu(acc)

    qc = conv_silu(xq, cwq_ref)
    kc = conv_silu(xk, cwk_ref)
    vc = conv_silu(xv, cwv_ref)

    ab = ab_ref[...]
    pre = ab + dtb_ref[...]
    softplus = jnp.maximum(pre, 0.0) + jnp.log1p(jnp.exp(-jnp.abs(pre)))
    g = -jnp.exp(alog_ref[...]) * softplus
    beta = jax.nn.sigmoid(ab)
    rowc = lax.broadcasted_iota(jnp.int32, (tt, LANES), 0) & (c - 1)
    gcum = g
    shift = 1
    while shift < c:
        gcum = gcum + jnp.where(rowc >= shift, pltpu.roll(gcum, shift, 0), 0.0)
        shift *= 2
    gcum_t = gcum.T
    beta_t = beta.T

    ri = lax.broadcasted_iota(jnp.int32, (c, c), 0)
    ci = lax.broadcasted_iota(jnp.int32, (c, c), 1)
    tril = ri >= ci
    strict = ri > ci
    eye = (ri == ci).astype(F32)

    for hh in range(GDN_GROUP):
        ls = slice(hh * hd, (hh + 1) * hd)
        qh, kh, vh = qc[:, ls], kc[:, ls], vc[:, ls]
        qn = (qh * lax.rsqrt(jnp.sum(qh * qh, axis=-1, keepdims=True) + NORM_EPS)) * (hd ** -0.5)
        kn = kh * lax.rsqrt(jnp.sum(kh * kh, axis=-1, keepdims=True) + NORM_EPS)
        state = state_ref[hh]
        for ch in range(tt // c):
            rs = slice(ch * c, (ch + 1) * c)
            g_col = gcum[rs, hh:hh + 1]
            g_row = gcum_t[hh:hh + 1, rs]
            b_col = beta[rs, GDN_GROUP + hh:GDN_GROUP + hh + 1]
            b_row = beta_t[GDN_GROUP + hh:GDN_GROUP + hh + 1, rs]
            g_last = gcum[ch * c + c - 1:ch * c + c, hh:hh + 1]
            kb = kn[rs].astype(BF16)
            qb = qn[rs].astype(BF16)
            vb = vh[rs].astype(BF16)
            decay = jnp.exp(jnp.where(tril, g_col - g_row, MASKED))
            low = jnp.where(strict, b_col * _dot_nt(kb, kb) * decay, 0.0)
            tinv = _unit_lower_inverse(low, eye) * b_row
            u = _dot(tinv.astype(BF16), vb)
            w = _dot((tinv * jnp.exp(g_row)).astype(BF16), kb)
            attn = jnp.where(tril, _dot_nt(qb, kb) * decay, 0.0)
            sb = state.astype(BF16)
            v_new = u - _dot(w.astype(BF16), sb)
            o_h = jnp.exp(g_col) * _dot(qb, sb) + _dot(attn.astype(BF16), v_new.astype(BF16))
            v_dec = (v_new * jnp.exp(g_last - g_col)).astype(BF16)
            state = state * jnp.exp(g_last) + _dot_tn(kb, v_dec)
            oacc[rs, ls] = o_h
        state_ref[hh] = state

    z = z_ref[...]
    ng = ng_ref[...]
    for hh in range(GDN_GROUP):
        ls = slice(hh * hd, (hh + 1) * hd)
        o_ref[:, ls] = ((_rms(oacc[:, ls]) * ng) * _silu(z[:, ls])).astype(o_ref.dtype)


def _gdn(p_gdn, p_ab, conv_w, alog2, dtb2, norm_g, *, batch, seq, tt=256):
    m = p_gdn.shape[0]
    gw = GDN_GROUP * GDN_HEAD_DIM
    n_groups = GDN_HEADS // GDN_GROUP
    nt = seq // tt

    def col(off):
        return pl.BlockSpec((tt, gw), lambda b, hg, t: (b * nt + t, off + hg))

    def cw(off):
        return pl.BlockSpec((GDN_CONV, gw), lambda b, hg, t: (0, off + hg))

    lane_spec = pl.BlockSpec((1, LANES), lambda b, hg, t: (0, hg))
    return pl.pallas_call(
        functools.partial(_gdn_kernel, tt=tt),
        out_shape=jax.ShapeDtypeStruct((m, GDN_HEADS * GDN_HEAD_DIM), BF16),
        grid=(batch, n_groups, nt),
        in_specs=[col(0), col(n_groups), col(2 * n_groups), col(3 * n_groups),
                  pl.BlockSpec((tt, LANES), lambda b, hg, t: (b * nt + t, hg)),
                  cw(0), cw(n_groups), cw(2 * n_groups),
                  lane_spec, lane_spec,
                  pl.BlockSpec((1, GDN_HEAD_DIM), lambda b, hg, t: (0, 0))],
        out_specs=pl.BlockSpec((tt, gw), lambda b, hg, t: (b * nt + t, hg)),
        scratch_shapes=[pltpu.VMEM((tt + GDN_HALO, gw), F32),
                        pltpu.VMEM((tt + GDN_HALO, gw), F32),
                        pltpu.VMEM((tt + GDN_HALO, gw), F32),
                        pltpu.VMEM((GDN_GROUP, GDN_HEAD_DIM, GDN_HEAD_DIM), F32),
                        pltpu.VMEM((tt, gw), F32)],
        compiler_params=_params("parallel", "parallel", "arbitrary"),
        name="gated_deltanet",
    )(p_gdn, p_gdn, p_gdn, p_gdn, p_ab, conv_w, conv_w, conv_w, alog2, dtb2,
      norm_g.reshape(1, GDN_HEAD_DIM))


def _moba_kernel(q_ref, k_ref, v_ref, o_ref, kmean_ref, *, nb):
    i = pl.program_id(2)
    blk = MOBA_BLOCK
    scale = MOBA_HEAD_DIM ** -0.5

    @pl.when(i == 0)
    def _():
        for n in range(nb):
            kblk = k_ref[n * blk:(n + 1) * blk, :].astype(F32)
            kmean_ref[n:n + 1, :] = jnp.sum(kblk, axis=0, keepdims=True) * (1.0 / blk)

    q = q_ref[...]
    kmean = kmean_ref[...]
    km_hi = kmean.astype(BF16)
    km_lo = (kmean - km_hi.astype(F32)).astype(BF16)
    gate = _dot_nt(q, km_hi) + _dot_nt(q, km_lo)

    lane = lax.broadcasted_iota(jnp.int32, (blk, nb), 1)
    rank = jnp.zeros((blk, nb), jnp.int32)
    for mth in range(nb):
        gm = gate[:, mth:mth + 1]
        beats = (gm > gate) | ((gm == gate) & (lane > mth))
        rank = rank + jnp.where(beats & (i > mth), 1, 0)
    sel = jnp.where((lane < i) & (rank < MOBA_TOPK), 1.0, 0.0)

    row0 = pl.multiple_of(i * blk, blk)
    k_own = k_ref[pl.ds(row0, blk), :]
    v_own = v_ref[pl.ds(row0, blk), :]
    ri = lax.broadcasted_iota(jnp.int32, (blk, blk), 0)
    ci = lax.broadcasted_iota(jnp.int32, (blk, blk), 1)
    s = jnp.where(ci <= ri, _dot_nt(q, k_own) * scale, MASKED)
    m0 = jnp.max(s, axis=-1, keepdims=True)
    p = jnp.exp(s - m0)
    l0 = jnp.sum(p, axis=-1, keepdims=True)
    acc0 = _dot(p.astype(BF16), v_own)

    def past_block(n, carry):
        m_run, l_run, acc = carry
        r0 = pl.multiple_of(n * blk, blk)
        k_n = k_ref[pl.ds(r0, blk), :]
        v_n = v_ref[pl.ds(r0, blk), :]
        chosen = jnp.sum(jnp.where(lane == n, sel, 0.0), axis=-1, keepdims=True) > 0.5
        s_n = jnp.where(chosen, _dot_nt(q, k_n) * scale, MASKED)
        m_new = jnp.maximum(m_run, jnp.max(s_n, axis=-1, keepdims=True))
        alpha = jnp.exp(m_run - m_new)
        p_n = jnp.exp(s_n - m_new)
        l_new = alpha * l_run + jnp.sum(p_n, axis=-1, keepdims=True)
        acc_new = alpha * acc + _dot(p_n.astype(BF16), v_n)
        return m_new, l_new, acc_new

    _, l_fin, acc_fin = lax.fori_loop(0, i, past_block, (m0, l0, acc0))
    o_ref[...] = (acc_fin / l_fin).astype(o_ref.dtype)


def _moba(p_qk, p_v, *, batch, seq):
    m = p_qk.shape[0]
    nb = seq // MOBA_BLOCK
    hd = MOBA_HEAD_DIM
    return pl.pallas_call(
        functools.partial(_moba_kernel, nb=nb),
        out_shape=jax.ShapeDtypeStruct((m, MOBA_HEADS * hd), BF16),
        grid=(batch, MOBA_HEADS, nb),
        in_specs=[pl.BlockSpec((MOBA_BLOCK, hd), lambda b, h, i: (b * nb + i, h)),
                  pl.BlockSpec((seq, hd), lambda b, h, i: (b, MOBA_HEADS + h)),
                  pl.BlockSpec((seq, hd), lambda b, h, i: (b, h))],
        out_specs=pl.BlockSpec((MOBA_BLOCK, hd), lambda b, h, i: (b * nb + i, h)),
        scratch_shapes=[pltpu.VMEM((nb, hd), F32)],
        compiler_params=_params("parallel", "parallel", "arbitrary"),
        name="moba_attention",
    )(p_qk, p_qk, p_v)


def _out_proj_kernel(x_ref, mod_ref, post_ref, yg_ref, ym_ref, wg_ref, wm_ref, o_ref, *, row):
    y = _dot(yg_ref[...], wg_ref[...]) + _dot(ym_ref[...], wm_ref[...])
    ga = mod_ref[0, row + 2:row + 3, :]
    o_ref[...] = x_ref[...] + ga * (_rms(y) * post_ref[...])


def _out_proj(x2, mod, row, post_g, y_gdn, y_moba, w_top, w_bot, *, seq, tm=512):
    m, d = x2.shape
    per_b = seq // tm
    kg, km = y_gdn.shape[1], y_moba.shape[1]
    return pl.pallas_call(
        functools.partial(_out_proj_kernel, row=row),
        out_shape=jax.ShapeDtypeStruct((m, d), F32),
        grid=(m // tm,),
        in_specs=[pl.BlockSpec((tm, d), lambda i: (i, 0)),
                  pl.BlockSpec((1, N_MOD, d), lambda i: (i // per_b, 0, 0)),
                  pl.BlockSpec((1, d), lambda i: (0, 0)),
                  pl.BlockSpec((tm, kg), lambda i: (i, 0)),
                  pl.BlockSpec((tm, km), lambda i: (i, 0)),
                  pl.BlockSpec((kg, d), lambda i: (0, 0)),
                  pl.BlockSpec((km, d), lambda i: (0, 0))],
        out_specs=pl.BlockSpec((tm, d), lambda i: (i, 0)),
        compiler_params=_params("parallel"),
        name="mixer_out_proj",
    )(x2, mod, post_g.reshape(1, d), y_gdn, y_moba, w_top, w_bot)


def _group_lanes(vec):
    groups = vec.reshape(GDN_HEADS // GDN_GROUP, GDN_GROUP)
    return jnp.pad(groups, ((0, 0), (0, LANES - GDN_GROUP))).reshape(1, -1)


def _gate_weights(w_a, w_b):
    d = w_a.shape[0]
    parts = []
    for hg in range(GDN_HEADS // GDN_GROUP):
        hs = slice(hg * GDN_GROUP, (hg + 1) * GDN_GROUP)
        parts += [w_a[:, hs], w_b[:, hs], jnp.zeros((d, LANES - 2 * GDN_GROUP), w_a.dtype)]
    return jnp.concatenate(parts, axis=1)


def kernel(x, c, w_ada, b_ada, ffn1_pre_g, ffn1_post_g, ffn1_w_gate, ffn1_w_up, ffn1_w_down, mix_pre_g, mix_post_g, w_in, gdn_conv_w, gdn_a_log, gdn_dt_bias, gdn_norm_g, w_out, ffn2_pre_g, ffn2_post_g, ffn2_w_gate, ffn2_w_up, ffn2_w_down):
    batch, seq, d = x.shape
    depth = w_ada.shape[0]
    gdn_w = GDN_HEADS * GDN_HEAD_DIM
    moba_w = MOBA_HEADS * MOBA_HEAD_DIM
    x2 = x.reshape(batch * seq, d)
    for l in range(depth):
        mod = _adaln(c, w_ada[l], b_ada[l]).reshape(batch, N_MOD, d)

        x2 = _ffn(x2, mod, 0, ffn1_pre_g[l], ffn1_post_g[l], ffn1_w_gate[l].astype(BF16),
                  ffn1_w_up[l].astype(BF16), ffn1_w_down[l].astype(BF16), seq=seq)

        wl = w_in[l]
        o_a = 4 * gdn_w
        o_b = o_a + GDN_HEADS
        o_m = o_b + GDN_HEADS
        w_gdn = wl[:, :o_a].astype(BF16)
        w_ab = _gate_weights(wl[:, o_a:o_b], wl[:, o_b:o_m]).astype(BF16)
        w_qk = wl[:, o_m:o_m + 2 * moba_w].astype(BF16)
        w_v = wl[:, o_m + 2 * moba_w:].astype(BF16)
        p_gdn, p_qk, p_v, p_ab = _in_proj(x2, mod, 3, mix_pre_g[l], w_gdn, w_qk, w_v, w_ab, seq=seq)
        y_gdn = _gdn(p_gdn, p_ab, gdn_conv_w[l], _group_lanes(gdn_a_log[l]),
                     _group_lanes(gdn_dt_bias[l]), gdn_norm_g[l], batch=batch, seq=seq)
        y_moba = _moba(p_qk, p_v, batch=batch, seq=seq)
        wo = w_out[l].astype(BF16)
        x2 = _out_proj(x2, mod, 3, mix_post_g[l], y_gdn, y_moba, wo[:gdn_w], wo[gdn_w:], seq=seq)

        x2 = _ffn(x2, mod, 6, ffn2_pre_g[l], ffn2_post_g[l], ffn2_w_gate[l].astype(BF16),
                  ffn2_w_up[l].astype(BF16), ffn2_w_down[l].astype(BF16), seq=seq)
    return x2.reshape(batch, seq, d)
```

```python
import functools

import jax
import jax.numpy as jnp
from jax import lax
from jax.experimental import pallas as pl
from jax.experimental.pallas import tpu as pltpu

F32 = jnp.float32
BF16 = jnp.bfloat16

NORM_EPS = 1e-6
N_MOD = 9
GDN_HEADS = 8
GDN_HEAD_DIM = 128
GDN_CONV = 4
GDN_CHUNK = 64
MOBA_HEADS = 8
MOBA_HEAD_DIM = 128
MOBA_BLOCK = 256
MOBA_TOPK = 3
ROPE_DIM = MOBA_HEAD_DIM // 4
ROPE_THETA = 500000.0

LANES = 128
SUBLANES = 8
VMEM_LIMIT = 56 * 1024 * 1024
MASKED = -1e30


def _dot(a, b):
    return jnp.dot(a, b, preferred_element_type=F32)


def _dot_nt(a, b):
    return lax.dot_general(a, b, (((1,), (1,)), ((), ())), preferred_element_type=F32)


def _dot_tn(a, b):
    return lax.dot_general(a, b, (((0,), (0,)), ((), ())), preferred_element_type=F32)


def _rms(x):
    return x * lax.rsqrt(jnp.mean(x * x, axis=-1, keepdims=True) + NORM_EPS)


def _silu(x):
    return x * jax.nn.sigmoid(x)


def _params(*sem):
    return pltpu.CompilerParams(dimension_semantics=sem, vmem_limit_bytes=VMEM_LIMIT)


def _adaln_kernel(c_ref, w_ref, b_ref, o_ref):
    o_ref[...] = _dot(_silu(c_ref[...]), w_ref[...]) + b_ref[...]


def _adaln(c, w, b, *, tn=1024):
    bsz, d = c.shape
    n = w.shape[1]
    rows = -(-bsz // SUBLANES) * SUBLANES
    c_pad = jnp.pad(c, ((0, rows - bsz), (0, 0)))
    out = pl.pallas_call(
        _adaln_kernel,
        out_shape=jax.ShapeDtypeStruct((rows, n), F32),
        grid=(n // tn,),
        in_specs=[pl.BlockSpec((rows, d), lambda j: (0, 0)),
                  pl.BlockSpec((d, tn), lambda j: (0, j)),
                  pl.BlockSpec((1, tn), lambda j: (0, j))],
        out_specs=pl.BlockSpec((rows, tn), lambda j: (0, j)),
        compiler_params=_params("parallel"),
        name="adaln",
    )(c_pad, w, b.reshape(1, n))
    return out[:bsz]


def _mod_norm(x_ref, mod_ref, g_ref, row):
    sh = mod_ref[0, row:row + 1, :]
    sc = mod_ref[0, row + 1:row + 2, :]
    return (_rms(x_ref[...]) * g_ref[...]) * (1.0 + sc) + sh


def _ffn_kernel(x_ref, mod_ref, pre_ref, post_ref, wg_ref, wu_ref, wd_ref, o_ref, h_ref, acc_ref,
                *, row):
    j = pl.program_id(1)

    @pl.when(j == 0)
    def _():
        h_ref[...] = _mod_norm(x_ref, mod_ref, pre_ref, row).astype(BF16)
        acc_ref[...] = jnp.zeros_like(acc_ref)

    h = h_ref[...]
    gate = _dot(h, wg_ref[...])
    up = _dot(h, wu_ref[...])
    act = (_silu(gate) * up).astype(BF16)
    acc_ref[...] += _dot(act, wd_ref[...])

    @pl.when(j == pl.num_programs(1) - 1)
    def _():
        ga = mod_ref[0, row + 2:row + 3, :]
        o_ref[...] = x_ref[...] + 0.5 * ga * (_rms(acc_ref[...]) * post_ref[...])


def _ffn(x2, mod, row, pre_g, post_g, wg, wu, wd, *, seq, tm=512, tf=512):
    m, d = x2.shape
    dff = wg.shape[1]
    per_b = seq // tm
    return pl.pallas_call(
        functools.partial(_ffn_kernel, row=row),
        out_shape=jax.ShapeDtypeStruct((m, d), F32),
        grid=(m // tm, dff // tf),
        in_specs=[pl.BlockSpec((tm, d), lambda i, j: (i, 0)),
                  pl.BlockSpec((1, N_MOD, d), lambda i, j: (i // per_b, 0, 0)),
                  pl.BlockSpec((1, d), lambda i, j: (0, 0)),
                  pl.BlockSpec((1, d), lambda i, j: (0, 0)),
                  pl.BlockSpec((d, tf), lambda i, j: (0, j)),
                  pl.BlockSpec((d, tf), lambda i, j: (0, j)),
                  pl.BlockSpec((tf, d), lambda i, j: (j, 0))],
        out_specs=pl.BlockSpec((tm, d), lambda i, j: (i, 0)),
        scratch_shapes=[pltpu.VMEM((tm, d), BF16), pltpu.VMEM((tm, d), F32)],
        compiler_params=_params("parallel", "arbitrary"),
        name="swiglu_ffn",
    )(x2, mod, pre_g.reshape(1, d), post_g.reshape(1, d), wg, wu, wd)


def _proj_gdn_kernel(x_ref, mod_ref, g_ref, w_ref, o_ref, h_ref, *, row):
    @pl.when(pl.program_id(1) == 0)
    def _():
        h_ref[...] = _mod_norm(x_ref, mod_ref, g_ref, row).astype(BF16)

    o_ref[...] = _dot(h_ref[...], w_ref[...])


def _proj_qk_kernel(x_ref, mod_ref, g_ref, w_ref, cos_ref, sa_ref, sb_ref, o_ref, h_ref, *, row):
    @pl.when(pl.program_id(1) == 0)
    def _():
        h_ref[...] = _mod_norm(x_ref, mod_ref, g_ref, row).astype(BF16)

    y = _dot(h_ref[...], w_ref[...])
    cos, sa, sb = cos_ref[...], sa_ref[...], sb_ref[...]
    half = ROPE_DIM // 2
    for hd in range(y.shape[1] // LANES):
        yh = y[:, hd * LANES:(hd + 1) * LANES]
        rot = (yh * cos + pltpu.roll(yh, LANES - half, 1) * sa + pltpu.roll(yh, half, 1) * sb)
        o_ref[:, hd * LANES:(hd + 1) * LANES] = rot.astype(o_ref.dtype)


def _proj_v_kernel(x_ref, mod_ref, g_ref, wv_ref, wab_ref, v_ref, ab_ref, *, row):
    h = _mod_norm(x_ref, mod_ref, g_ref, row).astype(BF16)
    v_ref[...] = _dot(h, wv_ref[...]).astype(v_ref.dtype)
    ab_ref[...] = _dot(h, wab_ref[...])


def _rope_tables(seq):
    half = ROPE_DIM // 2
    inv_freq = ROPE_THETA ** (-jnp.arange(0, ROPE_DIM, 2, dtype=F32) / ROPE_DIM)
    ang = jnp.arange(seq).astype(F32)[:, None] * inv_freq[None, :]
    cos, sin = jnp.cos(ang), jnp.sin(ang)
    rest = LANES - 2 * half
    zeros = jnp.zeros((seq, half), F32)
    tail0 = jnp.zeros((seq, rest), F32)
    cos_t = jnp.concatenate([cos, cos, jnp.ones((seq, rest), F32)], axis=1)
    sa_t = jnp.concatenate([-sin, zeros, tail0], axis=1)
    sb_t = jnp.concatenate([zeros, sin, tail0], axis=1)
    return cos_t, sa_t, sb_t


def _in_proj(x2, mod, row, g, w_gdn, w_qk, w_v, w_ab, *, seq, tm=512, tn=1024):
    m, d = x2.shape
    per_b = seq // tm
    x_spec2 = pl.BlockSpec((tm, d), lambda i, j: (i, 0))
    mod_spec2 = pl.BlockSpec((1, N_MOD, d), lambda i, j: (i // per_b, 0, 0))
    g_spec2 = pl.BlockSpec((1, d), lambda i, j: (0, 0))
    g2 = g.reshape(1, d)

    n_gdn = w_gdn.shape[1]
    p_gdn = pl.pallas_call(
        functools.partial(_proj_gdn_kernel, row=row),
        out_shape=jax.ShapeDtypeStruct((m, n_gdn), F32),
        grid=(m // tm, n_gdn // tn),
        in_specs=[x_spec2, mod_spec2, g_spec2, pl.BlockSpec((d, tn), lambda i, j: (0, j))],
        out_specs=pl.BlockSpec((tm, tn), lambda i, j: (i, j)),
        scratch_shapes=[pltpu.VMEM((tm, d), BF16)],
        compiler_params=_params("parallel", "arbitrary"),
        name="proj_gdn",
    )(x2, mod, g2, w_gdn)

    n_qk = w_qk.shape[1]
    cos_t, sa_t, sb_t = _rope_tables(seq)
    tab_spec = pl.BlockSpec((tm, LANES), lambda i, j: (i % per_b, 0))
    p_qk = pl.pallas_call(
        functools.partial(_proj_qk_kernel, row=row),
        out_shape=jax.ShapeDtypeStruct((m, n_qk), BF16),
        grid=(m // tm, n_qk // tn),
        in_specs=[x_spec2, mod_spec2, g_spec2, pl.BlockSpec((d, tn), lambda i, j: (0, j)),
                  tab_spec, tab_spec, tab_spec],
        out_specs=pl.BlockSpec((tm, tn), lambda i, j: (i, j)),
        scratch_shapes=[pltpu.VMEM((tm, d), BF16)],
        compiler_params=_params("parallel", "arbitrary"),
        name="proj_moba_qk",
    )(x2, mod, g2, w_qk, cos_t, sa_t, sb_t)

    n_v, n_ab = w_v.shape[1], w_ab.shape[1]
    p_v, p_ab = pl.pallas_call(
        functools.partial(_proj_v_kernel, row=row),
        out_shape=(jax.ShapeDtypeStruct((m, n_v), BF16), jax.ShapeDtypeStruct((m, n_ab), F32)),
        grid=(m // tm,),
        in_specs=[pl.BlockSpec((tm, d), lambda i: (i, 0)),
                  pl.BlockSpec((1, N_MOD, d), lambda i: (i // per_b, 0, 0)),
                  pl.BlockSpec((1, d), lambda i: (0, 0)),
                  pl.BlockSpec((d, n_v), lambda i: (0, 0)),
                  pl.BlockSpec((d, n_ab), lambda i: (0, 0))],
        out_specs=(pl.BlockSpec((tm, n_v), lambda i: (i, 0)),
                   pl.BlockSpec((tm, n_ab), lambda i: (i, 0))),
        compiler_params=_params("parallel"),
        name="proj_moba_v_gates",
    )(x2, mod, g2, w_v, w_ab)
    return p_gdn, p_qk, p_v, p_ab


GDN_GROUP = 4
GDN_HALO = SUBLANES


def _gdn_kernel(q_ref, k_ref, v_ref, z_ref, ab_ref, cwq_ref, cwk_ref, cwv_ref, alog_ref, dtb_ref,
                ng_ref, o_ref, xq, xk, xv, state_ref, oacc, *, tt):
    t = pl.program_id(2)
    c = GDN_CHUNK
    hd = GDN_HEAD_DIM
    halo = GDN_HALO

    @pl.when(t == 0)
    def _():
        for buf in (xq, xk, xv):
            buf[0:halo, :] = jnp.zeros((halo, buf.shape[1]), F32)
        state_ref[...] = jnp.zeros_like(state_ref)

    @pl.when(t > 0)
    def _():
        for buf in (xq, xk, xv):
            buf[0:halo, :] = buf[tt:tt + halo, :]

    xq[halo:halo + tt, :] = q_ref[...]
    xk[halo:halo + tt, :] = k_ref[...]
    xv[halo:halo + tt, :] = v_ref[...]

    def conv_silu(buf, cw_ref):
        base = halo - (GDN_CONV - 1)
        acc = buf[pl.ds(base, tt), :] * cw_ref[0:1, :]
        for j in range(1, GDN_CONV):
            acc = acc + buf[pl.ds(base + j, tt), :] * cw_ref[j:j + 1, :]
        return _silu(acc)

    qc = conv_silu(xq, cwq_ref)
    kc = conv_silu(xk, cwk_ref)
    vc = conv_silu(xv, cwv_ref)

    ab = ab_ref[...]
    pre = ab + dtb_ref[...]
    softplus = jnp.maximum(pre, 0.0) + jnp.log1p(jnp.exp(-jnp.abs(pre)))
    g = -jnp.exp(alog_ref[...]) * softplus
    beta = jax.nn.sigmoid(ab)
    rowc = lax.broadcasted_iota(jnp.int32, (tt, LANES), 0) & (c - 1)
    gcum = g
    shift = 1
    while shift < c:
        gcum = gcum + jnp.where(rowc >= shift, pltpu.roll(gcum, shift, 0), 0.0)
        shift *= 2
    gcum_t = gcum.T
    beta_t = beta.T

    ri = lax.broadcasted_iota(jnp.int32, (c, c), 0)
    ci = lax.broadcasted_iota(jnp.int32, (c, c), 1)
    tril = ri >= ci
    strict = ri > ci
    eye = (ri == ci).astype(F32)

    n_ch = tt // c
    heads = range(GDN_GROUP)
    items = [(hh, ch) for hh in heads for ch in range(n_ch)]
    qn, kn, vn = [], [], []
    for hh in heads:
        ls = slice(hh * hd, (hh + 1) * hd)
        qh, kh = qc[:, ls], kc[:, ls]
        qn.append(((qh * lax.rsqrt(jnp.sum(qh * qh, axis=-1, keepdims=True) + NORM_EPS))
                   * (hd ** -0.5)).astype(BF16))
        kn.append((kh * lax.rsqrt(jnp.sum(kh * kh, axis=-1, keepdims=True) + NORM_EPS)).astype(BF16))
        vn.append(vc[:, ls].astype(BF16))

    def rows(ch):
        return slice(ch * c, (ch + 1) * c)

    kb = {(hh, ch): kn[hh][rows(ch)] for hh, ch in items}
    qb = {(hh, ch): qn[hh][rows(ch)] for hh, ch in items}
    vb = {(hh, ch): vn[hh][rows(ch)] for hh, ch in items}
    g_col = {(hh, ch): gcum[rows(ch), hh:hh + 1] for hh, ch in items}
    g_row = {(hh, ch): gcum_t[hh:hh + 1, rows(ch)] for hh, ch in items}
    b_col = {(hh, ch): beta[rows(ch), GDN_GROUP + hh:GDN_GROUP + hh + 1] for hh, ch in items}
    b_row = {(hh, ch): beta_t[GDN_GROUP + hh:GDN_GROUP + hh + 1, rows(ch)] for hh, ch in items}
    g_last = {(hh, ch): gcum[ch * c + c - 1:ch * c + c, hh:hh + 1] for hh, ch in items}

    decay = {it: jnp.exp(jnp.where(tril, g_col[it] - g_row[it], MASKED)) for it in items}
    kk = {it: _dot_nt(kb[it], kb[it]) for it in items}
    qk = {it: _dot_nt(qb[it], kb[it]) for it in items}
    low = {it: jnp.where(strict, b_col[it] * kk[it] * decay[it], 0.0) for it in items}
    attn = {it: jnp.where(tril, qk[it] * decay[it], 0.0).astype(BF16) for it in items}

    inv = {it: eye - jnp.where((ri >> 1) == (ci >> 1), low[it], 0.0) for it in items}
    size = 4
    while size <= c:
        bits = size.bit_length() - 1
        off = ((ri >> bits) == (ci >> bits)) & ((ri >> (bits - 1)) != (ci >> (bits - 1)))
        ib = {it: inv[it].astype(BF16) for it in items}
        right = {it: _dot(jnp.where(off, low[it], 0.0).astype(BF16), ib[it]).astype(BF16)
                 for it in items}
        inv = {it: inv[it] - _dot(ib[it], right[it]) for it in items}
        size *= 2

    tb = {it: inv[it] * b_row[it] for it in items}
    u = {it: _dot(tb[it].astype(BF16), vb[it]) for it in items}
    w = {it: _dot((tb[it] * jnp.exp(g_row[it])).astype(BF16), kb[it]).astype(BF16) for it in items}
    e_col = {it: jnp.exp(g_col[it]) for it in items}
    e_dec = {it: jnp.exp(g_last[it] - g_col[it]) for it in items}
    e_last = {it: jnp.exp(g_last[it]) for it in items}

    state = [state_ref[hh] for hh in heads]
    for ch in range(n_ch):
        sb = [state[hh].astype(BF16) for hh in heads]
        w_s = [_dot(w[(hh, ch)], sb[hh]) for hh in heads]
        q_s = [_dot(qb[(hh, ch)], sb[hh]) for hh in heads]
        v_new = [u[(hh, ch)] - w_s[hh] for hh in heads]
        v_dec = [(v_new[hh] * e_dec[(hh, ch)]).astype(BF16) for hh in heads]
        upd = [_dot_tn(kb[(hh, ch)], v_dec[hh]) for hh in heads]
        intra = [_dot(attn[(hh, ch)], v_new[hh].astype(BF16)) for hh in heads]
        for hh in heads:
            oacc[rows(ch), hh * hd:(hh + 1) * hd] = e_col[(hh, ch)] * q_s[hh] + intra[hh]
        state = [state[hh] * e_last[(hh, ch)] + upd[hh] for hh in heads]
    for hh in heads:
        state_ref[hh] = state[hh]

    z = z_ref[...]
    ng = ng_ref[...]
    for hh in range(GDN_GROUP):
        ls = slice(hh * hd, (hh + 1) * hd)
        o_ref[:, ls] = ((_rms(oacc[:, ls]) * ng) * _silu(z[:, ls])).astype(o_ref.dtype)


def _gdn(p_gdn, p_ab, conv_w, alog2, dtb2, norm_g, *, batch, seq, tt=256):
    m = p_gdn.shape[0]
    gw = GDN_GROUP * GDN_HEAD_DIM
    n_groups = GDN_HEADS // GDN_GROUP
    nt = seq // tt

    def col(off):
        return pl.BlockSpec((tt, gw), lambda b, hg, t: (b * nt + t, off + hg))

    def cw(off):
        return pl.BlockSpec((GDN_CONV, gw), lambda b, hg, t: (0, off + hg))

    lane_spec = pl.BlockSpec((1, LANES), lambda b, hg, t: (0, hg))
    return pl.pallas_call(
        functools.partial(_gdn_kernel, tt=tt),
        out_shape=jax.ShapeDtypeStruct((m, GDN_HEADS * GDN_HEAD_DIM), BF16),
        grid=(batch, n_groups, nt),
        in_specs=[col(0), col(n_groups), col(2 * n_groups), col(3 * n_groups),
                  pl.BlockSpec((tt, LANES), lambda b, hg, t: (b * nt + t, hg)),
                  cw(0), cw(n_groups), cw(2 * n_groups),
                  lane_spec, lane_spec,
                  pl.BlockSpec((1, GDN_HEAD_DIM), lambda b, hg, t: (0, 0))],
        out_specs=pl.BlockSpec((tt, gw), lambda b, hg, t: (b * nt + t, hg)),
        scratch_shapes=[pltpu.VMEM((tt + GDN_HALO, gw), F32),
                        pltpu.VMEM((tt + GDN_HALO, gw), F32),
                        pltpu.VMEM((tt + GDN_HALO, gw), F32),
                        pltpu.VMEM((GDN_GROUP, GDN_HEAD_DIM, GDN_HEAD_DIM), F32),
                        pltpu.VMEM((tt, gw), F32)],
        compiler_params=_params("parallel", "parallel", "arbitrary"),
        name="gated_deltanet",
    )(p_gdn, p_gdn, p_gdn, p_gdn, p_ab, conv_w, conv_w, conv_w, alog2, dtb2,
      norm_g.reshape(1, GDN_HEAD_DIM))


def _moba_kernel(q_ref, k_ref, v_ref, o_ref, kmean_ref, *, nb):
    i = pl.program_id(2)
    blk = MOBA_BLOCK
    scale = MOBA_HEAD_DIM ** -0.5

    @pl.when(i == 0)
    def _():
        for n in range(nb):
            kblk = k_ref[n * blk:(n + 1) * blk, :].astype(F32)
            kmean_ref[n:n + 1, :] = jnp.sum(kblk, axis=0, keepdims=True) * (1.0 / blk)

    ri = lax.broadcasted_iota(jnp.int32, (blk, blk), 0)
    ci = lax.broadcasted_iota(jnp.int32, (blk, blk), 1)
    causal = ci <= ri

    def attend(own):
        q = q_ref[...]
        n_keys = (own + 1) * blk
        s = _dot_nt(q, k_ref[0:n_keys, :]) * scale
        pieces = [s[:, n * blk:(n + 1) * blk] for n in range(own)]
        if own > MOBA_TOPK:
            kmean = kmean_ref[0:own, :]
            km_hi = kmean.astype(BF16)
            km_lo = (kmean - km_hi.astype(F32)).astype(BF16)
            gate = _dot_nt(q, km_hi) + _dot_nt(q, km_lo)
            lane = lax.broadcasted_iota(jnp.int32, (blk, own), 1)
            rank = jnp.zeros((blk, own), jnp.int32)
            for mth in range(own):
                gm = gate[:, mth:mth + 1]
                beats = (gm > gate) | ((gm == gate) & (lane > mth))
                rank = rank + jnp.where(beats, 1, 0)
            sel = rank < MOBA_TOPK
            pieces = [jnp.where(sel[:, n:n + 1], pieces[n], MASKED) for n in range(own)]
        pieces.append(jnp.where(causal, s[:, own * blk:], MASKED))
        m_run = functools.reduce(jnp.maximum, pieces)
        m_run = jnp.max(m_run, axis=-1, keepdims=True)
        probs = [jnp.exp(pc - m_run) for pc in pieces]
        denom = jnp.sum(functools.reduce(lambda a, b: a + b, probs), axis=-1, keepdims=True)
        pcat = jnp.concatenate([pr.astype(BF16) for pr in probs], axis=1)
        acc = _dot(pcat, v_ref[0:n_keys, :])
        o_ref[...] = (acc / denom).astype(o_ref.dtype)

    for own in range(nb):
        pl.when(i == own)(functools.partial(attend, own))


def _moba(p_qk, p_v, *, batch, seq):
    m = p_qk.shape[0]
    nb = seq // MOBA_BLOCK
    hd = MOBA_HEAD_DIM
    return pl.pallas_call(
        functools.partial(_moba_kernel, nb=nb),
        out_shape=jax.ShapeDtypeStruct((m, MOBA_HEADS * hd), BF16),
        grid=(batch, MOBA_HEADS, nb),
        in_specs=[pl.BlockSpec((MOBA_BLOCK, hd), lambda b, h, i: (b * nb + i, h)),
                  pl.BlockSpec((seq, hd), lambda b, h, i: (b, MOBA_HEADS + h)),
                  pl.BlockSpec((seq, hd), lambda b, h, i: (b, h))],
        out_specs=pl.BlockSpec((MOBA_BLOCK, hd), lambda b, h, i: (b * nb + i, h)),
        scratch_shapes=[pltpu.VMEM((nb, hd), F32)],
        compiler_params=_params("parallel", "parallel", "arbitrary"),
        name="moba_attention",
    )(p_qk, p_qk, p_v)


def _out_proj_kernel(x_ref, mod_ref, post_ref, yg_ref, ym_ref, wg_ref, wm_ref, o_ref, *, row):
    y = _dot(yg_ref[...], wg_ref[...]) + _dot(ym_ref[...], wm_ref[...])
    ga = mod_ref[0, row + 2:row + 3, :]
    o_ref[...] = x_ref[...] + ga * (_rms(y) * post_ref[...])


def _out_proj(x2, mod, row, post_g, y_gdn, y_moba, w_top, w_bot, *, seq, tm=512):
    m, d = x2.shape
    per_b = seq // tm
    kg, km = y_gdn.shape[1], y_moba.shape[1]
    return pl.pallas_call(
        functools.partial(_out_proj_kernel, row=row),
        out_shape=jax.ShapeDtypeStruct((m, d), F32),
        grid=(m // tm,),
        in_specs=[pl.BlockSpec((tm, d), lambda i: (i, 0)),
                  pl.BlockSpec((1, N_MOD, d), lambda i: (i // per_b, 0, 0)),
                  pl.BlockSpec((1, d), lambda i: (0, 0)),
                  pl.BlockSpec((tm, kg), lambda i: (i, 0)),
                  pl.BlockSpec((tm, km), lambda i: (i, 0)),
                  pl.BlockSpec((kg, d), lambda i: (0, 0)),
                  pl.BlockSpec((km, d), lambda i: (0, 0))],
        out_specs=pl.BlockSpec((tm, d), lambda i: (i, 0)),
        compiler_params=_params("parallel"),
        name="mixer_out_proj",
    )(x2, mod, post_g.reshape(1, d), y_gdn, y_moba, w_top, w_bot)


def _group_lanes(vec):
    groups = vec.reshape(GDN_HEADS // GDN_GROUP, GDN_GROUP)
    return jnp.pad(groups, ((0, 0), (0, LANES - GDN_GROUP))).reshape(1, -1)


def _gate_weights(w_a, w_b):
    d = w_a.shape[0]
    parts = []
    for hg in range(GDN_HEADS // GDN_GROUP):
        hs = slice(hg * GDN_GROUP, (hg + 1) * GDN_GROUP)
        parts += [w_a[:, hs], w_b[:, hs], jnp.zeros((d, LANES - 2 * GDN_GROUP), w_a.dtype)]
    return jnp.concatenate(parts, axis=1)


def kernel(x, c, w_ada, b_ada, ffn1_pre_g, ffn1_post_g, ffn1_w_gate, ffn1_w_up, ffn1_w_down, mix_pre_g, mix_post_g, w_in, gdn_conv_w, gdn_a_log, gdn_dt_bias, gdn_norm_g, w_out, ffn2_pre_g, ffn2_post_g, ffn2_w_gate, ffn2_w_up, ffn2_w_down):
    batch, seq, d = x.shape
    depth = w_ada.shape[0]
    gdn_w = GDN_HEADS * GDN_HEAD_DIM
    moba_w = MOBA_HEADS * MOBA_HEAD_DIM
    x2 = x.reshape(batch * seq, d)
    for l in range(depth):
        mod = _adaln(c, w_ada[l], b_ada[l]).reshape(batch, N_MOD, d)

        x2 = _ffn(x2, mod, 0, ffn1_pre_g[l], ffn1_post_g[l], ffn1_w_gate[l].astype(BF16),
                  ffn1_w_up[l].astype(BF16), ffn1_w_down[l].astype(BF16), seq=seq)

        wl = w_in[l]
        o_a = 4 * gdn_w
        o_b = o_a + GDN_HEADS
        o_m = o_b + GDN_HEADS
        w_gdn = wl[:, :o_a].astype(BF16)
        w_ab = _gate_weights(wl[:, o_a:o_b], wl[:, o_b:o_m]).astype(BF16)
        w_qk = wl[:, o_m:o_m + 2 * moba_w].astype(BF16)
        w_v = wl[:, o_m + 2 * moba_w:].astype(BF16)
        p_gdn, p_qk, p_v, p_ab = _in_proj(x2, mod, 3, mix_pre_g[l], w_gdn, w_qk, w_v, w_ab, seq=seq)
        y_gdn = _gdn(p_gdn, p_ab, gdn_conv_w[l], _group_lanes(gdn_a_log[l]),
                     _group_lanes(gdn_dt_bias[l]), gdn_norm_g[l], batch=batch, seq=seq)
        y_moba = _moba(p_qk, p_v, batch=batch, seq=seq)
        wo = w_out[l].astype(BF16)
        x2 = _out_proj(x2, mod, 3, mix_post_g[l], y_gdn, y_moba, wo[:gdn_w], wo[gdn_w:], seq=seq)

        x2 = _ffn(x2, mod, 6, ffn2_pre_g[l], ffn2_post_g[l], ffn2_w_gate[l].astype(BF16),
                  ffn2_w_up[l].astype(BF16), ffn2_w_down[l].astype(BF16), seq=seq)
    return x2.reshape(batch, seq, d)
```

```python
import functools

import jax
import jax.numpy as jnp
from jax import lax
from jax.experimental import pallas as pl
from jax.experimental.pallas import tpu as pltpu

F32 = jnp.float32
BF16 = jnp.bfloat16

NORM_EPS = 1e-6
N_MOD = 9
GDN_HEADS = 8
GDN_HEAD_DIM = 128
GDN_CONV = 4
GDN_CHUNK = 64
MOBA_HEADS = 8
MOBA_HEAD_DIM = 128
MOBA_BLOCK = 256
MOBA_TOPK = 3
ROPE_DIM = MOBA_HEAD_DIM // 4
ROPE_THETA = 500000.0

LANES = 128
SUBLANES = 8
VMEM_LIMIT = 56 * 1024 * 1024
MASKED = -1e30


def _dot(a, b):
    return jnp.dot(a, b, preferred_element_type=F32)


def _dot_nt(a, b):
    return lax.dot_general(a, b, (((1,), (1,)), ((), ())), preferred_element_type=F32)


def _dot_tn(a, b):
    return lax.dot_general(a, b, (((0,), (0,)), ((), ())), preferred_element_type=F32)


def _rms(x):
    return x * lax.rsqrt(jnp.mean(x * x, axis=-1, keepdims=True) + NORM_EPS)


def _silu(x):
    return x * jax.nn.sigmoid(x)


def _params(*sem):
    return pltpu.CompilerParams(dimension_semantics=sem, vmem_limit_bytes=VMEM_LIMIT)


def _adaln_kernel(c_ref, w_ref, b_ref, o_ref):
    o_ref[...] = _dot(_silu(c_ref[...]), w_ref[...]) + b_ref[...]


def _adaln(c, w, b, *, tn=1024):
    bsz, d = c.shape
    n = w.shape[1]
    rows = -(-bsz // SUBLANES) * SUBLANES
    c_pad = jnp.pad(c, ((0, rows - bsz), (0, 0)))
    out = pl.pallas_call(
        _adaln_kernel,
        out_shape=jax.ShapeDtypeStruct((rows, n), F32),
        grid=(n // tn,),
        in_specs=[pl.BlockSpec((rows, d), lambda j: (0, 0)),
                  pl.BlockSpec((d, tn), lambda j: (0, j)),
                  pl.BlockSpec((1, tn), lambda j: (0, j))],
        out_specs=pl.BlockSpec((rows, tn), lambda j: (0, j)),
        compiler_params=_params("parallel"),
        name="adaln",
    )(c_pad, w, b.reshape(1, n))
    return out[:bsz]


def _mod_norm(x, mod_ref, g_ref, row):
    sh = mod_ref[0, row:row + 1, :]
    sc = mod_ref[0, row + 1:row + 2, :]
    return (_rms(x) * g_ref[...]) * (1.0 + sc) + sh


def _ffn_kernel(*refs, row, emit_next):
    if emit_next:
        (x_ref, mod_ref, pre_ref, post_ref, next_ref, wg_ref, wu_ref, wd_ref,
         o_ref, hn_ref, h_ref, acc_ref) = refs
    else:
        x_ref, mod_ref, pre_ref, post_ref, wg_ref, wu_ref, wd_ref, o_ref, h_ref, acc_ref = refs
    j = pl.program_id(1)

    @pl.when(j == 0)
    def _():
        h_ref[...] = _mod_norm(x_ref[...], mod_ref, pre_ref, row).astype(BF16)
        acc_ref[...] = jnp.zeros_like(acc_ref)

    h = h_ref[...]
    gate = _dot(h, wg_ref[...])
    up = _dot(h, wu_ref[...])
    act = (_silu(gate) * up).astype(BF16)
    acc_ref[...] += _dot(act, wd_ref[...])

    @pl.when(j == pl.num_programs(1) - 1)
    def _():
        ga = mod_ref[0, row + 2:row + 3, :]
        out = x_ref[...] + 0.5 * ga * (_rms(acc_ref[...]) * post_ref[...])
        o_ref[...] = out
        if emit_next:
            hn_ref[...] = _mod_norm(out, mod_ref, next_ref, row + 3).astype(BF16)


def _ffn(x2, mod, row, pre_g, post_g, wg, wu, wd, *, seq, next_g=None, tm=512, tf=512):
    m, d = x2.shape
    dff = wg.shape[1]
    per_b = seq // tm
    emit_next = next_g is not None
    row_spec = pl.BlockSpec((1, d), lambda i, j: (0, 0))
    tile_spec = pl.BlockSpec((tm, d), lambda i, j: (i, 0))
    gains = [pre_g.reshape(1, d), post_g.reshape(1, d)] + ([next_g.reshape(1, d)] if emit_next else [])
    out_shape = [jax.ShapeDtypeStruct((m, d), F32)] + ([jax.ShapeDtypeStruct((m, d), BF16)] if emit_next else [])
    outs = pl.pallas_call(
        functools.partial(_ffn_kernel, row=row, emit_next=emit_next),
        out_shape=out_shape,
        grid=(m // tm, dff // tf),
        in_specs=[tile_spec, pl.BlockSpec((1, N_MOD, d), lambda i, j: (i // per_b, 0, 0))]
                 + [row_spec] * len(gains)
                 + [pl.BlockSpec((d, tf), lambda i, j: (0, j)),
                    pl.BlockSpec((d, tf), lambda i, j: (0, j)),
                    pl.BlockSpec((tf, d), lambda i, j: (j, 0))],
        out_specs=[tile_spec] * len(out_shape),
        scratch_shapes=[pltpu.VMEM((tm, d), BF16), pltpu.VMEM((tm, d), F32)],
        compiler_params=_params("parallel", "arbitrary"),
        name="swiglu_ffn",
    )(x2, mod, *gains, wg, wu, wd)
    return outs if emit_next else outs[0]


def _proj_gdn_kernel(h_ref, w_ref, o_ref):
    o_ref[...] = _dot(h_ref[...], w_ref[...])


def _proj_qk_kernel(h_ref, w_ref, cos_ref, sa_ref, sb_ref, o_ref):
    y = _dot(h_ref[...], w_ref[...])
    cos, sa, sb = cos_ref[...], sa_ref[...], sb_ref[...]
    half = ROPE_DIM // 2
    for hd in range(y.shape[1] // LANES):
        yh = y[:, hd * LANES:(hd + 1) * LANES]
        rot = (yh * cos + pltpu.roll(yh, LANES - half, 1) * sa + pltpu.roll(yh, half, 1) * sb)
        o_ref[:, hd * LANES:(hd + 1) * LANES] = rot.astype(o_ref.dtype)


def _proj_v_kernel(h_ref, wv_ref, wab_ref, v_ref, ab_ref):
    h = h_ref[...]
    v_ref[...] = _dot(h, wv_ref[...]).astype(v_ref.dtype)
    ab_ref[...] = _dot(h, wab_ref[...])


def _rope_tables(seq):
    half = ROPE_DIM // 2
    inv_freq = ROPE_THETA ** (-jnp.arange(0, ROPE_DIM, 2, dtype=F32) / ROPE_DIM)
    ang = jnp.arange(seq).astype(F32)[:, None] * inv_freq[None, :]
    cos, sin = jnp.cos(ang), jnp.sin(ang)
    rest = LANES - 2 * half
    zeros = jnp.zeros((seq, half), F32)
    tail0 = jnp.zeros((seq, rest), F32)
    cos_t = jnp.concatenate([cos, cos, jnp.ones((seq, rest), F32)], axis=1)
    sa_t = jnp.concatenate([-sin, zeros, tail0], axis=1)
    sb_t = jnp.concatenate([zeros, sin, tail0], axis=1)
    return cos_t, sa_t, sb_t


def _in_proj(h2, w_gdn, w_qk, w_v, w_ab, *, seq, tm=512, tn=1024):
    m, d = h2.shape
    per_b = seq // tm
    h_spec = pl.BlockSpec((tm, d), lambda j, i: (i, 0))
    w_spec = pl.BlockSpec((d, tn), lambda j, i: (0, j))
    o_spec = pl.BlockSpec((tm, tn), lambda j, i: (i, j))

    n_gdn = w_gdn.shape[1]
    p_gdn = pl.pallas_call(
        _proj_gdn_kernel,
        out_shape=jax.ShapeDtypeStruct((m, n_gdn), F32),
        grid=(n_gdn // tn, m // tm),
        in_specs=[h_spec, w_spec],
        out_specs=o_spec,
        compiler_params=_params("parallel", "parallel"),
        name="proj_gdn",
    )(h2, w_gdn)

    n_qk = w_qk.shape[1]
    cos_t, sa_t, sb_t = _rope_tables(seq)
    tab_spec = pl.BlockSpec((tm, LANES), lambda j, i: (i % per_b, 0))
    p_qk = pl.pallas_call(
        _proj_qk_kernel,
        out_shape=jax.ShapeDtypeStruct((m, n_qk), BF16),
        grid=(n_qk // tn, m // tm),
        in_specs=[h_spec, w_spec, tab_spec, tab_spec, tab_spec],
        out_specs=o_spec,
        compiler_params=_params("parallel", "parallel"),
        name="proj_moba_qk",
    )(h2, w_qk, cos_t, sa_t, sb_t)

    n_v, n_ab = w_v.shape[1], w_ab.shape[1]
    p_v, p_ab = pl.pallas_call(
        _proj_v_kernel,
        out_shape=(jax.ShapeDtypeStruct((m, n_v), BF16), jax.ShapeDtypeStruct((m, n_ab), F32)),
        grid=(m // tm,),
        in_specs=[pl.BlockSpec((tm, d), lambda i: (i, 0)),
                  pl.BlockSpec((d, n_v), lambda i: (0, 0)),
                  pl.BlockSpec((d, n_ab), lambda i: (0, 0))],
        out_specs=(pl.BlockSpec((tm, n_v), lambda i: (i, 0)),
                   pl.BlockSpec((tm, n_ab), lambda i: (i, 0))),
        compiler_params=_params("parallel"),
        name="proj_moba_v_gates",
    )(h2, w_v, w_ab)
    return p_gdn, p_qk, p_v, p_ab


GDN_GROUP = 4
GDN_HALO = SUBLANES


def _gdn_kernel(q_ref, k_ref, v_ref, z_ref, ab_ref, cwq_ref, cwk_ref, cwv_ref, alog_ref, dtb_ref,
                ng_ref, o_ref, xq, xk, xv, state_ref, oacc, *, tt):
    t = pl.program_id(2)
    c = GDN_CHUNK
    hd = GDN_HEAD_DIM
    halo = GDN_HALO

    @pl.when(t == 0)
    def _():
        for buf in (xq, xk, xv):
            buf[0:halo, :] = jnp.zeros((halo, buf.shape[1]), F32)
        state_ref[...] = jnp.zeros_like(state_ref)

    @pl.when(t > 0)
    def _():
        for buf in (xq, xk, xv):
            buf[0:halo, :] = buf[tt:tt + halo, :]

    xq[halo:halo + tt, :] = q_ref[...]
    xk[halo:halo + tt, :] = k_ref[...]
    xv[halo:halo + tt, :] = v_ref[...]

    def conv_silu(buf, cw_ref):
        base = halo - (GDN_CONV - 1)
        acc = buf[pl.ds(base, tt), :] * cw_ref[0:1, :]
        for j in range(1, GDN_CONV):
            acc = acc + buf[pl.ds(base + j, tt), :] * cw_ref[j:j + 1, :]
        return _silu(acc)

    qc = conv_silu(xq, cwq_ref)
    kc = conv_silu(xk, cwk_ref)
    vc = conv_silu(xv, cwv_ref)

    ab = ab_ref[...]
    pre = ab + dtb_ref[...]
    softplus = jnp.maximum(pre, 0.0) + jnp.log1p(jnp.exp(-jnp.abs(pre)))
    g = -jnp.exp(alog_ref[...]) * softplus
    beta = jax.nn.sigmoid(ab)
    rowc = lax.broadcasted_iota(jnp.int32, (tt, LANES), 0) & (c - 1)
    gcum = g
    shift = 1
    while shift < c:
        gcum = gcum + jnp.where(rowc >= shift, pltpu.roll(gcum, shift, 0), 0.0)
        shift *= 2
    gcum_t = gcum.T
    beta_t = beta.T

    ri = lax.broadcasted_iota(jnp.int32, (c, c), 0)
    ci = lax.broadcasted_iota(jnp.int32, (c, c), 1)
    tril = ri >= ci
    strict = ri > ci
    eye = (ri == ci).astype(F32)

    n_ch = tt // c
    heads = range(GDN_GROUP)
    items = [(hh, ch) for hh in heads for ch in range(n_ch)]
    qn, kn, vn = [], [], []
    for hh in heads:
        ls = slice(hh * hd, (hh + 1) * hd)
        qh, kh = qc[:, ls], kc[:, ls]
        qn.append(((qh * lax.rsqrt(jnp.sum(qh * qh, axis=-1, keepdims=True) + NORM_EPS))
                   * (hd ** -0.5)).astype(BF16))
        kn.append((kh * lax.rsqrt(jnp.sum(kh * kh, axis=-1, keepdims=True) + NORM_EPS)).astype(BF16))
        vn.append(vc[:, ls].astype(BF16))

    def rows(ch):
        return slice(ch * c, (ch + 1) * c)

    kb = {(hh, ch): kn[hh][rows(ch)] for hh, ch in items}
    qb = {(hh, ch): qn[hh][rows(ch)] for hh, ch in items}
    vb = {(hh, ch): vn[hh][rows(ch)] for hh, ch in items}
    g_col = {(hh, ch): gcum[rows(ch), hh:hh + 1] for hh, ch in items}
    g_row = {(hh, ch): gcum_t[hh:hh + 1, rows(ch)] for hh, ch in items}
    b_col = {(hh, ch): beta[rows(ch), GDN_GROUP + hh:GDN_GROUP + hh + 1] for hh, ch in items}
    b_row = {(hh, ch): beta_t[GDN_GROUP + hh:GDN_GROUP + hh + 1, rows(ch)] for hh, ch in items}
    g_last = {(hh, ch): gcum[ch * c + c - 1:ch * c + c, hh:hh + 1] for hh, ch in items}

    decay = {it: jnp.exp(jnp.where(tril, g_col[it] - g_row[it], MASKED)) for it in items}
    kk = {it: _dot_nt(kb[it], kb[it]) for it in items}
    qk = {it: _dot_nt(qb[it], kb[it]) for it in items}
    low = {it: jnp.where(strict, b_col[it] * kk[it] * decay[it], 0.0) for it in items}
    attn = {it: jnp.where(tril, qk[it] * decay[it], 0.0).astype(BF16) for it in items}

    inv = {it: eye - jnp.where((ri >> 1) == (ci >> 1), low[it], 0.0) for it in items}
    size = 4
    while size <= c:
        bits = size.bit_length() - 1
        off = ((ri >> bits) == (ci >> bits)) & ((ri >> (bits - 1)) != (ci >> (bits - 1)))
        ib = {it: inv[it].astype(BF16) for it in items}
        right = {it: _dot(jnp.where(off, low[it], 0.0).astype(BF16), ib[it]).astype(BF16)
                 for it in items}
        inv = {it: inv[it] - _dot(ib[it], right[it]) for it in items}
        size *= 2

    tb = {it: inv[it] * b_row[it] for it in items}
    u = {it: _dot(tb[it].astype(BF16), vb[it]) for it in items}
    w = {it: _dot((tb[it] * jnp.exp(g_row[it])).astype(BF16), kb[it]).astype(BF16) for it in items}
    e_col = {it: jnp.exp(g_col[it]) for it in items}
    e_dec = {it: jnp.exp(g_last[it] - g_col[it]) for it in items}
    e_last = {it: jnp.exp(g_last[it]) for it in items}

    state = [state_ref[hh] for hh in heads]
    for ch in range(n_ch):
        sb = [state[hh].astype(BF16) for hh in heads]
        w_s = [_dot(w[(hh, ch)], sb[hh]) for hh in heads]
        q_s = [_dot(qb[(hh, ch)], sb[hh]) for hh in heads]
        v_new = [u[(hh, ch)] - w_s[hh] for hh in heads]
        v_dec = [(v_new[hh] * e_dec[(hh, ch)]).astype(BF16) for hh in heads]
        upd = [_dot_tn(kb[(hh, ch)], v_dec[hh]) for hh in heads]
        intra = [_dot(attn[(hh, ch)], v_new[hh].astype(BF16)) for hh in heads]
        for hh in heads:
            oacc[rows(ch), hh * hd:(hh + 1) * hd] = e_col[(hh, ch)] * q_s[hh] + intra[hh]
        state = [state[hh] * e_last[(hh, ch)] + upd[hh] for hh in heads]
    for hh in heads:
        state_ref[hh] = state[hh]

    z = z_ref[...]
    ng = ng_ref[...]
    for hh in range(GDN_GROUP):
        ls = slice(hh * hd, (hh + 1) * hd)
        o_ref[:, ls] = ((_rms(oacc[:, ls]) * ng) * _silu(z[:, ls])).astype(o_ref.dtype)


def _gdn(p_gdn, p_ab, conv_w, alog2, dtb2, norm_g, *, batch, seq, tt=256):
    m = p_gdn.shape[0]
    gw = GDN_GROUP * GDN_HEAD_DIM
    n_groups = GDN_HEADS // GDN_GROUP
    nt = seq // tt

    def col(off):
        return pl.BlockSpec((tt, gw), lambda b, hg, t: (b * nt + t, off + hg))

    def cw(off):
        return pl.BlockSpec((GDN_CONV, gw), lambda b, hg, t: (0, off + hg))

    lane_spec = pl.BlockSpec((1, LANES), lambda b, hg, t: (0, hg))
    return pl.pallas_call(
        functools.partial(_gdn_kernel, tt=tt),
        out_shape=jax.ShapeDtypeStruct((m, GDN_HEADS * GDN_HEAD_DIM), BF16),
        grid=(batch, n_groups, nt),
        in_specs=[col(0), col(n_groups), col(2 * n_groups), col(3 * n_groups),
                  pl.BlockSpec((tt, LANES), lambda b, hg, t: (b * nt + t, hg)),
                  cw(0), cw(n_groups), cw(2 * n_groups),
                  lane_spec, lane_spec,
                  pl.BlockSpec((1, GDN_HEAD_DIM), lambda b, hg, t: (0, 0))],
        out_specs=pl.BlockSpec((tt, gw), lambda b, hg, t: (b * nt + t, hg)),
        scratch_shapes=[pltpu.VMEM((tt + GDN_HALO, gw), F32),
                        pltpu.VMEM((tt + GDN_HALO, gw), F32),
                        pltpu.VMEM((tt + GDN_HALO, gw), F32),
                        pltpu.VMEM((GDN_GROUP, GDN_HEAD_DIM, GDN_HEAD_DIM), F32),
                        pltpu.VMEM((tt, gw), F32)],
        compiler_params=_params("parallel", "parallel", "arbitrary"),
        name="gated_deltanet",
    )(p_gdn, p_gdn, p_gdn, p_gdn, p_ab, conv_w, conv_w, conv_w, alog2, dtb2,
      norm_g.reshape(1, GDN_HEAD_DIM))


def _moba_kernel(q_ref, k_ref, v_ref, o_ref, kmean_ref, *, nb):
    blk = MOBA_BLOCK
    exp2_scale = (MOBA_HEAD_DIM ** -0.5) * 1.4426950408889634

    for n in range(nb - 1):
        kblk = k_ref[n * blk:(n + 1) * blk, :].astype(F32)
        kmean_ref[n:n + 1, :] = jnp.sum(kblk, axis=0, keepdims=True) * (1.0 / blk)

    ri = lax.broadcasted_iota(jnp.int32, (blk, blk), 0)
    ci = lax.broadcasted_iota(jnp.int32, (blk, blk), 1)
    causal = ci <= ri
    owns = range(nb)

    q = [q_ref[own * blk:(own + 1) * blk, :] for own in owns]
    s = [_dot_nt(q[own], k_ref[0:(own + 1) * blk, :]) for own in owns]

    sel = {}
    for own in owns:
        if own <= MOBA_TOPK:
            continue
        kmean = kmean_ref[0:own, :]
        km_hi = kmean.astype(BF16)
        km_lo = (kmean - km_hi.astype(F32)).astype(BF16)
        gate = _dot_nt(q[own], km_hi) + _dot_nt(q[own], km_lo)
        lane = lax.broadcasted_iota(jnp.int32, (blk, own), 1)
        rank = jnp.zeros((blk, own), jnp.int32)
        for mth in range(own):
            gm = gate[:, mth:mth + 1]
            beats = (gm > gate) | ((gm == gate) & (lane > mth))
            rank = rank + jnp.where(beats, 1, 0)
        sel[own] = rank < MOBA_TOPK

    pieces = []
    for own in owns:
        past = [s[own][:, n * blk:(n + 1) * blk] for n in range(own)]
        if own in sel:
            past = [jnp.where(sel[own][:, n:n + 1], past[n], MASKED) for n in range(own)]
        pieces.append(past + [jnp.where(causal, s[own][:, own * blk:], MASKED)])
    m_run = [jnp.max(functools.reduce(jnp.maximum, pieces[own]), axis=-1, keepdims=True)
             for own in owns]
    probs = [[jnp.exp2((pc - m_run[own]) * exp2_scale) for pc in pieces[own]] for own in owns]
    denom = [jnp.sum(functools.reduce(lambda a, b: a + b, probs[own]), axis=-1, keepdims=True)
             for own in owns]
    acc = [_dot(jnp.concatenate([pr.astype(BF16) for pr in probs[own]], axis=1),
                v_ref[0:(own + 1) * blk, :]) for own in owns]
    for own in owns:
        o_ref[own * blk:(own + 1) * blk, :] = (acc[own] / denom[own]).astype(o_ref.dtype)


def _moba(p_qk, p_v, *, batch, seq):
    m = p_qk.shape[0]
    nb = seq // MOBA_BLOCK
    hd = MOBA_HEAD_DIM
    return pl.pallas_call(
        functools.partial(_moba_kernel, nb=nb),
        out_shape=jax.ShapeDtypeStruct((m, MOBA_HEADS * hd), BF16),
        grid=(batch, MOBA_HEADS),
        in_specs=[pl.BlockSpec((seq, hd), lambda b, h: (b, h)),
                  pl.BlockSpec((seq, hd), lambda b, h: (b, MOBA_HEADS + h)),
                  pl.BlockSpec((seq, hd), lambda b, h: (b, h))],
        out_specs=pl.BlockSpec((seq, hd), lambda b, h: (b, h)),
        scratch_shapes=[pltpu.VMEM((nb, hd), F32)],
        compiler_params=_params("parallel", "parallel"),
        name="moba_attention",
    )(p_qk, p_qk, p_v)


def _out_proj_kernel(x_ref, mod_ref, post_ref, yg_ref, ym_ref, wg_ref, wm_ref, o_ref, *, row):
    y = _dot(yg_ref[...], wg_ref[...]) + _dot(ym_ref[...], wm_ref[...])
    ga = mod_ref[0, row + 2:row + 3, :]
    o_ref[...] = x_ref[...] + ga * (_rms(y) * post_ref[...])


def _out_proj(x2, mod, row, post_g, y_gdn, y_moba, w_top, w_bot, *, seq, tm=512):
    m, d = x2.shape
    per_b = seq // tm
    kg, km = y_gdn.shape[1], y_moba.shape[1]
    return pl.pallas_call(
        functools.partial(_out_proj_kernel, row=row),
        out_shape=jax.ShapeDtypeStruct((m, d), F32),
        grid=(m // tm,),
        in_specs=[pl.BlockSpec((tm, d), lambda i: (i, 0)),
                  pl.BlockSpec((1, N_MOD, d), lambda i: (i // per_b, 0, 0)),
                  pl.BlockSpec((1, d), lambda i: (0, 0)),
                  pl.BlockSpec((tm, kg), lambda i: (i, 0)),
                  pl.BlockSpec((tm, km), lambda i: (i, 0)),
                  pl.BlockSpec((kg, d), lambda i: (0, 0)),
                  pl.BlockSpec((km, d), lambda i: (0, 0))],
        out_specs=pl.BlockSpec((tm, d), lambda i: (i, 0)),
        compiler_params=_params("parallel"),
        name="mixer_out_proj",
    )(x2, mod, post_g.reshape(1, d), y_gdn, y_moba, w_top, w_bot)


def _group_lanes(vec):
    groups = vec.reshape(GDN_HEADS // GDN_GROUP, GDN_GROUP)
    return jnp.pad(groups, ((0, 0), (0, LANES - GDN_GROUP))).reshape(1, -1)


def _gate_weights(w_a, w_b):
    d = w_a.shape[0]
    parts = []
    for hg in range(GDN_HEADS // GDN_GROUP):
        hs = slice(hg * GDN_GROUP, (hg + 1) * GDN_GROUP)
        parts += [w_a[:, hs], w_b[:, hs], jnp.zeros((d, LANES - 2 * GDN_GROUP), w_a.dtype)]
    return jnp.concatenate(parts, axis=1)


def kernel(x, c, w_ada, b_ada, ffn1_pre_g, ffn1_post_g, ffn1_w_gate, ffn1_w_up, ffn1_w_down, mix_pre_g, mix_post_g, w_in, gdn_conv_w, gdn_a_log, gdn_dt_bias, gdn_norm_g, w_out, ffn2_pre_g, ffn2_post_g, ffn2_w_gate, ffn2_w_up, ffn2_w_down):
    batch, seq, d = x.shape
    depth = w_ada.shape[0]
    gdn_w = GDN_HEADS * GDN_HEAD_DIM
    moba_w = MOBA_HEADS * MOBA_HEAD_DIM
    x2 = x.reshape(batch * seq, d)
    for l in range(depth):
        mod = _adaln(c, w_ada[l], b_ada[l]).reshape(batch, N_MOD, d)

        x2, h2 = _ffn(x2, mod, 0, ffn1_pre_g[l], ffn1_post_g[l], ffn1_w_gate[l].astype(BF16),
                      ffn1_w_up[l].astype(BF16), ffn1_w_down[l].astype(BF16), seq=seq,
                      next_g=mix_pre_g[l])

        wl = w_in[l]
        o_a = 4 * gdn_w
        o_b = o_a + GDN_HEADS
        o_m = o_b + GDN_HEADS
        w_gdn = wl[:, :o_a].astype(BF16)
        w_ab = _gate_weights(wl[:, o_a:o_b], wl[:, o_b:o_m]).astype(BF16)
        w_qk = wl[:, o_m:o_m + 2 * moba_w].astype(BF16)
        w_v = wl[:, o_m + 2 * moba_w:].astype(BF16)
        p_gdn, p_qk, p_v, p_ab = _in_proj(h2, w_gdn, w_qk, w_v, w_ab, seq=seq)
        y_gdn = _gdn(p_gdn, p_ab, gdn_conv_w[l], _group_lanes(gdn_a_log[l]),
                     _group_lanes(gdn_dt_bias[l]), gdn_norm_g[l], batch=batch, seq=seq)
        y_moba = _moba(p_qk, p_v, batch=batch, seq=seq)
        wo = w_out[l].astype(BF16)
        x2 = _out_proj(x2, mod, 3, mix_post_g[l], y_gdn, y_moba, wo[:gdn_w], wo[gdn_w:], seq=seq)

        x2 = _ffn(x2, mod, 6, ffn2_pre_g[l], ffn2_post_g[l], ffn2_w_gate[l].astype(BF16),
                  ffn2_w_up[l].astype(BF16), ffn2_w_down[l].astype(BF16), seq=seq)
    return x2.reshape(batch, seq, d)
```

```python
import functools

import jax
import jax.numpy as jnp
from jax import lax
from jax.experimental import pallas as pl
from jax.experimental.pallas import tpu as pltpu

F32 = jnp.float32
BF16 = jnp.bfloat16

NORM_EPS = 1e-6
N_MOD = 9
GDN_HEADS = 8
GDN_HEAD_DIM = 128
GDN_CONV = 4
GDN_CHUNK = 128
MOBA_HEADS = 8
MOBA_HEAD_DIM = 128
MOBA_BLOCK = 256
MOBA_TOPK = 3
ROPE_DIM = MOBA_HEAD_DIM // 4
ROPE_THETA = 500000.0

LANES = 128
SUBLANES = 8
VMEM_LIMIT = 56 * 1024 * 1024
MASKED = -1e30


def _dot(a, b):
    return jnp.dot(a, b, preferred_element_type=F32)


def _dot_nt(a, b):
    return lax.dot_general(a, b, (((1,), (1,)), ((), ())), preferred_element_type=F32)


def _dot_tn(a, b):
    return lax.dot_general(a, b, (((0,), (0,)), ((), ())), preferred_element_type=F32)


def _rms(x):
    return x * lax.rsqrt(jnp.mean(x * x, axis=-1, keepdims=True) + NORM_EPS)


def _silu(x):
    return x * jax.nn.sigmoid(x)


def _params(*sem):
    return pltpu.CompilerParams(dimension_semantics=sem, vmem_limit_bytes=VMEM_LIMIT)


def _adaln_kernel(c_ref, w_ref, b_ref, o_ref):
    o_ref[...] = _dot(_silu(c_ref[...]), w_ref[...]) + b_ref[...]


def _adaln(c, w, b, *, tn=1024):
    bsz, d = c.shape
    n = w.shape[1]
    rows = -(-bsz // SUBLANES) * SUBLANES
    c_pad = jnp.pad(c, ((0, rows - bsz), (0, 0)))
    out = pl.pallas_call(
        _adaln_kernel,
        out_shape=jax.ShapeDtypeStruct((rows, n), F32),
        grid=(n // tn,),
        in_specs=[pl.BlockSpec((rows, d), lambda j: (0, 0)),
                  pl.BlockSpec((d, tn), lambda j: (0, j)),
                  pl.BlockSpec((1, tn), lambda j: (0, j))],
        out_specs=pl.BlockSpec((rows, tn), lambda j: (0, j)),
        compiler_params=_params("parallel"),
        name="adaln",
    )(c_pad, w, b.reshape(1, n))
    return out[:bsz]


def _mod_norm(x, mod_ref, g_ref, row):
    sh = mod_ref[0, row:row + 1, :]
    sc = mod_ref[0, row + 1:row + 2, :]
    return (_rms(x) * g_ref[...]) * (1.0 + sc) + sh


def _ffn_kernel(*refs, row, emit_next, n_blocks):
    if emit_next:
        (x_ref, mod_ref, pre_ref, post_ref, next_ref, wga_ref, wua_ref, wda_ref,
         wgb_ref, wub_ref, wdb_ref, o_ref, hn_ref, h_ref) = refs
    else:
        (x_ref, mod_ref, pre_ref, post_ref, wga_ref, wua_ref, wda_ref,
         wgb_ref, wub_ref, wdb_ref, o_ref, h_ref) = refs
    j = pl.program_id(1)
    paired_steps = n_blocks // 2

    @pl.when(j == 0)
    def _():
        h_ref[...] = _mod_norm(x_ref[...], mod_ref, pre_ref, row).astype(BF16)
        o_ref[...] = jnp.zeros_like(o_ref)

    def hidden(wg_ref, wu_ref):
        h = h_ref[...]
        return (_silu(_dot(h, wg_ref[...])) * _dot(h, wu_ref[...])).astype(BF16)

    @pl.when(j < paired_steps)
    def _():
        act_a = hidden(wga_ref, wua_ref)
        act_b = hidden(wgb_ref, wub_ref)
        o_ref[...] += _dot(act_a, wda_ref[...]) + _dot(act_b, wdb_ref[...])

    if n_blocks % 2:
        @pl.when(j == paired_steps)
        def _():
            o_ref[...] += _dot(hidden(wga_ref, wua_ref), wda_ref[...])

    @pl.when(j == pl.num_programs(1) - 1)
    def _():
        ga = mod_ref[0, row + 2:row + 3, :]
        out = x_ref[...] + 0.5 * ga * (_rms(o_ref[...]) * post_ref[...])
        o_ref[...] = out
        if emit_next:
            hn_ref[...] = _mod_norm(out, mod_ref, next_ref, row + 3).astype(BF16)


def _ffn(x2, mod, row, pre_g, post_g, wg, wu, wd, *, seq, next_g=None, tm=512, tf=512):
    m, d = x2.shape
    dff = wg.shape[1]
    per_b = seq // tm
    n_blocks = dff // tf
    emit_next = next_g is not None
    row_spec = pl.BlockSpec((1, d), lambda i, j: (0, 0))
    tile_spec = pl.BlockSpec((tm, d), lambda i, j: (i, 0))
    gains = [pre_g.reshape(1, d), post_g.reshape(1, d)] + ([next_g.reshape(1, d)] if emit_next else [])
    out_shape = [jax.ShapeDtypeStruct((m, d), F32)] + ([jax.ShapeDtypeStruct((m, d), BF16)] if emit_next else [])

    def blk(j, which):
        return jnp.minimum(2 * j + which, n_blocks - 1)

    def w_specs(which):
        return [pl.BlockSpec((d, tf), lambda i, j: (0, blk(j, which))),
                pl.BlockSpec((d, tf), lambda i, j: (0, blk(j, which))),
                pl.BlockSpec((tf, d), lambda i, j: (blk(j, which), 0))]

    outs = pl.pallas_call(
        functools.partial(_ffn_kernel, row=row, emit_next=emit_next, n_blocks=n_blocks),
        out_shape=out_shape,
        grid=(m // tm, (n_blocks + 1) // 2),
        in_specs=[tile_spec, pl.BlockSpec((1, N_MOD, d), lambda i, j: (i // per_b, 0, 0))]
                 + [row_spec] * len(gains) + w_specs(0) + w_specs(1),
        out_specs=[tile_spec] * len(out_shape),
        scratch_shapes=[pltpu.VMEM((tm, d), BF16)],
        compiler_params=_params("parallel", "arbitrary"),
        name="swiglu_ffn",
    )(x2, mod, *gains, wg, wu, wd, wg, wu, wd)
    return outs if emit_next else outs[0]


def _proj_gdn_kernel(h_ref, w_ref, o_ref):
    o_ref[...] = _dot(h_ref[...], w_ref[...])


def _proj_qk_kernel(h_ref, w_ref, cos_ref, sa_ref, sb_ref, o_ref):
    y = _dot(h_ref[...], w_ref[...])
    cos, sa, sb = cos_ref[...], sa_ref[...], sb_ref[...]
    half = ROPE_DIM // 2
    for hd in range(y.shape[1] // LANES):
        yh = y[:, hd * LANES:(hd + 1) * LANES]
        rot = (yh * cos + pltpu.roll(yh, LANES - half, 1) * sa + pltpu.roll(yh, half, 1) * sb)
        o_ref[:, hd * LANES:(hd + 1) * LANES] = rot.astype(o_ref.dtype)


def _proj_v_kernel(h_ref, wv_ref, wab_ref, v_ref, ab_ref):
    h = h_ref[...]
    v_ref[...] = _dot(h, wv_ref[...]).astype(v_ref.dtype)
    ab_ref[...] = _dot(h, wab_ref[...])


def _rope_tables(seq):
    half = ROPE_DIM // 2
    inv_freq = ROPE_THETA ** (-jnp.arange(0, ROPE_DIM, 2, dtype=F32) / ROPE_DIM)
    ang = jnp.arange(seq).astype(F32)[:, None] * inv_freq[None, :]
    cos, sin = jnp.cos(ang), jnp.sin(ang)
    rest = LANES - 2 * half
    zeros = jnp.zeros((seq, half), F32)
    tail0 = jnp.zeros((seq, rest), F32)
    cos_t = jnp.concatenate([cos, cos, jnp.ones((seq, rest), F32)], axis=1)
    sa_t = jnp.concatenate([-sin, zeros, tail0], axis=1)
    sb_t = jnp.concatenate([zeros, sin, tail0], axis=1)
    return cos_t, sa_t, sb_t


def _in_proj(h2, w_gdn, w_qk, w_v, w_ab, *, seq, tm=512, tn=1024):
    m, d = h2.shape
    per_b = seq // tm
    h_spec = pl.BlockSpec((tm, d), lambda j, i: (i, 0))
    w_spec = pl.BlockSpec((d, tn), lambda j, i: (0, j))
    o_spec = pl.BlockSpec((tm, tn), lambda j, i: (i, j))

    n_gdn = w_gdn.shape[1]
    p_gdn = pl.pallas_call(
        _proj_gdn_kernel,
        out_shape=jax.ShapeDtypeStruct((m, n_gdn), F32),
        grid=(n_gdn // tn, m // tm),
        in_specs=[h_spec, w_spec],
        out_specs=o_spec,
        compiler_params=_params("parallel", "parallel"),
        name="proj_gdn",
    )(h2, w_gdn)

    n_qk = w_qk.shape[1]
    cos_t, sa_t, sb_t = _rope_tables(seq)
    tab_spec = pl.BlockSpec((tm, LANES), lambda j, i: (i % per_b, 0))
    p_qk = pl.pallas_call(
        _proj_qk_kernel,
        out_shape=jax.ShapeDtypeStruct((m, n_qk), BF16),
        grid=(n_qk // tn, m // tm),
        in_specs=[h_spec, w_spec, tab_spec, tab_spec, tab_spec],
        out_specs=o_spec,
        compiler_params=_params("parallel", "parallel"),
        name="proj_moba_qk",
    )(h2, w_qk, cos_t, sa_t, sb_t)

    n_v, n_ab = w_v.shape[1], w_ab.shape[1]
    p_v, p_ab = pl.pallas_call(
        _proj_v_kernel,
        out_shape=(jax.ShapeDtypeStruct((m, n_v), BF16), jax.ShapeDtypeStruct((m, n_ab), F32)),
        grid=(m // tm,),
        in_specs=[pl.BlockSpec((tm, d), lambda i: (i, 0)),
                  pl.BlockSpec((d, n_v), lambda i: (0, 0)),
                  pl.BlockSpec((d, n_ab), lambda i: (0, 0))],
        out_specs=(pl.BlockSpec((tm, n_v), lambda i: (i, 0)),
                   pl.BlockSpec((tm, n_ab), lambda i: (i, 0))),
        compiler_params=_params("parallel"),
        name="proj_moba_v_gates",
    )(h2, w_v, w_ab)
    return p_gdn, p_qk, p_v, p_ab


GDN_GROUP = 4
GDN_HALO = SUBLANES


def _gdn_kernel(q_ref, k_ref, v_ref, z_ref, ab_ref, cwq_ref, cwk_ref, cwv_ref, alog_ref, dtb_ref,
                ng_ref, o_ref, xq, xk, xv, state_ref, oacc, *, tt):
    t = pl.program_id(2)
    c = GDN_CHUNK
    hd = GDN_HEAD_DIM
    halo = GDN_HALO

    @pl.when(t == 0)
    def _():
        for buf in (xq, xk, xv):
            buf[0:halo, :] = jnp.zeros((halo, buf.shape[1]), F32)
        state_ref[...] = jnp.zeros_like(state_ref)

    @pl.when(t > 0)
    def _():
        for buf in (xq, xk, xv):
            buf[0:halo, :] = buf[tt:tt + halo, :]

    xq[halo:halo + tt, :] = q_ref[...]
    xk[halo:halo + tt, :] = k_ref[...]
    xv[halo:halo + tt, :] = v_ref[...]

    def conv_silu(buf, cw_ref):
        base = halo - (GDN_CONV - 1)
        acc = buf[pl.ds(base, tt), :] * cw_ref[0:1, :]
        for j in range(1, GDN_CONV):
            acc = acc + buf[pl.ds(base + j, tt), :] * cw_ref[j:j + 1, :]
        return _silu(acc)

    qc = conv_silu(xq, cwq_ref)
    kc = conv_silu(xk, cwk_ref)
    vc = conv_silu(xv, cwv_ref)

    ab = ab_ref[...]
    pre = ab + dtb_ref[...]
    softplus = jnp.maximum(pre, 0.0) + jnp.log1p(jnp.exp(-jnp.abs(pre)))
    g = -jnp.exp(alog_ref[...]) * softplus
    beta = jax.nn.sigmoid(ab)
    rowc = lax.broadcasted_iota(jnp.int32, (tt, LANES), 0) & (c - 1)
    gcum = g
    shift = 1
    while shift < c:
        gcum = gcum + jnp.where(rowc >= shift, pltpu.roll(gcum, shift, 0), 0.0)
        shift *= 2
    gcum_t = gcum.T
    beta_t = beta.T

    ri = lax.broadcasted_iota(jnp.int32, (c, c), 0)
    ci = lax.broadcasted_iota(jnp.int32, (c, c), 1)
    tril = ri >= ci
    strict = ri > ci
    eye = (ri == ci).astype(F32)

    n_ch = tt // c
    heads = range(GDN_GROUP)
    items = [(hh, ch) for hh in heads for ch in range(n_ch)]
    qn, kn, vn = [], [], []
    for hh in heads:
        ls = slice(hh * hd, (hh + 1) * hd)
        qh, kh = qc[:, ls], kc[:, ls]
        qn.append(((qh * lax.rsqrt(jnp.sum(qh * qh, axis=-1, keepdims=True) + NORM_EPS))
                   * (hd ** -0.5)).astype(BF16))
        kn.append((kh * lax.rsqrt(jnp.sum(kh * kh, axis=-1, keepdims=True) + NORM_EPS)).astype(BF16))
        vn.append(vc[:, ls].astype(BF16))

    def rows(ch):
        return slice(ch * c, (ch + 1) * c)

    kb = {(hh, ch): kn[hh][rows(ch)] for hh, ch in items}
    qb = {(hh, ch): qn[hh][rows(ch)] for hh, ch in items}
    vb = {(hh, ch): vn[hh][rows(ch)] for hh, ch in items}
    g_col = {(hh, ch): gcum[rows(ch), hh:hh + 1] for hh, ch in items}
    g_row = {(hh, ch): gcum_t[hh:hh + 1, rows(ch)] for hh, ch in items}
    b_col = {(hh, ch): beta[rows(ch), GDN_GROUP + hh:GDN_GROUP + hh + 1] for hh, ch in items}
    b_row = {(hh, ch): beta_t[GDN_GROUP + hh:GDN_GROUP + hh + 1, rows(ch)] for hh, ch in items}
    g_last = {(hh, ch): gcum[ch * c + c - 1:ch * c + c, hh:hh + 1] for hh, ch in items}

    decay = {it: jnp.exp(jnp.where(tril, g_col[it] - g_row[it], MASKED)) for it in items}
    kk = {it: _dot_nt(kb[it], kb[it]) for it in items}
    qk = {it: _dot_nt(qb[it], kb[it]) for it in items}
    low = {it: jnp.where(strict, b_col[it] * kk[it] * decay[it], 0.0) for it in items}
    attn = {it: jnp.where(tril, qk[it] * decay[it], 0.0).astype(BF16) for it in items}

    inv = {it: eye - jnp.where((ri >> 1) == (ci >> 1), low[it], 0.0) for it in items}
    size = 4
    while size <= c:
        bits = size.bit_length() - 1
        off = ((ri >> bits) == (ci >> bits)) & ((ri >> (bits - 1)) != (ci >> (bits - 1)))
        ib = {it: inv[it].astype(BF16) for it in items}
        right = {it: _dot(jnp.where(off, low[it], 0.0).astype(BF16), ib[it]).astype(BF16)
                 for it in items}
        inv = {it: inv[it] - _dot(ib[it], right[it]) for it in items}
        size *= 2

    tb = {it: inv[it] * b_row[it] for it in items}
    u = {it: _dot(tb[it].astype(BF16), vb[it]) for it in items}
    w = {it: _dot((tb[it] * jnp.exp(g_row[it])).astype(BF16), kb[it]).astype(BF16) for it in items}
    e_col = {it: jnp.exp(g_col[it]) for it in items}
    e_dec = {it: jnp.exp(g_last[it] - g_col[it]) for it in items}
    e_last = {it: jnp.exp(g_last[it]) for it in items}

    state = [state_ref[hh] for hh in heads]
    for ch in range(n_ch):
        sb = [state[hh].astype(BF16) for hh in heads]
        w_s = [_dot(w[(hh, ch)], sb[hh]) for hh in heads]
        q_s = [_dot(qb[(hh, ch)], sb[hh]) for hh in heads]
        v_new = [u[(hh, ch)] - w_s[hh] for hh in heads]
        v_dec = [(v_new[hh] * e_dec[(hh, ch)]).astype(BF16) for hh in heads]
        upd = [_dot_tn(kb[(hh, ch)], v_dec[hh]) for hh in heads]
        intra = [_dot(attn[(hh, ch)], v_new[hh].astype(BF16)) for hh in heads]
        for hh in heads:
            oacc[rows(ch), hh * hd:(hh + 1) * hd] = e_col[(hh, ch)] * q_s[hh] + intra[hh]
        state = [state[hh] * e_last[(hh, ch)] + upd[hh] for hh in heads]
    for hh in heads:
        state_ref[hh] = state[hh]

    z = z_ref[...]
    ng = ng_ref[...]
    for hh in range(GDN_GROUP):
        ls = slice(hh * hd, (hh + 1) * hd)
        o_ref[:, ls] = ((_rms(oacc[:, ls]) * ng) * _silu(z[:, ls])).astype(o_ref.dtype)


def _gdn(p_gdn, p_ab, conv_w, alog2, dtb2, norm_g, *, batch, seq, tt=512):
    m = p_gdn.shape[0]
    gw = GDN_GROUP * GDN_HEAD_DIM
    n_groups = GDN_HEADS // GDN_GROUP
    nt = seq // tt

    def col(off):
        return pl.BlockSpec((tt, gw), lambda b, hg, t: (b * nt + t, off + hg))

    def cw(off):
        return pl.BlockSpec((GDN_CONV, gw), lambda b, hg, t: (0, off + hg))

    lane_spec = pl.BlockSpec((1, LANES), lambda b, hg, t: (0, hg))
    return pl.pallas_call(
        functools.partial(_gdn_kernel, tt=tt),
        out_shape=jax.ShapeDtypeStruct((m, GDN_HEADS * GDN_HEAD_DIM), BF16),
        grid=(batch, n_groups, nt),
        in_specs=[col(0), col(n_groups), col(2 * n_groups), col(3 * n_groups),
                  pl.BlockSpec((tt, LANES), lambda b, hg, t: (b * nt + t, hg)),
                  cw(0), cw(n_groups), cw(2 * n_groups),
                  lane_spec, lane_spec,
                  pl.BlockSpec((1, GDN_HEAD_DIM), lambda b, hg, t: (0, 0))],
        out_specs=pl.BlockSpec((tt, gw), lambda b, hg, t: (b * nt + t, hg)),
        scratch_shapes=[pltpu.VMEM((tt + GDN_HALO, gw), F32),
                        pltpu.VMEM((tt + GDN_HALO, gw), F32),
                        pltpu.VMEM((tt + GDN_HALO, gw), F32),
                        pltpu.VMEM((GDN_GROUP, GDN_HEAD_DIM, GDN_HEAD_DIM), F32),
                        pltpu.VMEM((tt, gw), F32)],
        compiler_params=_params("parallel", "parallel", "arbitrary"),
        name="gated_deltanet",
    )(p_gdn, p_gdn, p_gdn, p_gdn, p_ab, conv_w, conv_w, conv_w, alog2, dtb2,
      norm_g.reshape(1, GDN_HEAD_DIM))


def _moba_kernel(q_ref, k_ref, v_ref, o_ref, kmean_ref, *, nb):
    blk = MOBA_BLOCK
    exp2_scale = (MOBA_HEAD_DIM ** -0.5) * 1.4426950408889634

    for n in range(nb - 1):
        kblk = k_ref[n * blk:(n + 1) * blk, :].astype(F32)
        kmean_ref[n:n + 1, :] = jnp.sum(kblk, axis=0, keepdims=True) * (1.0 / blk)

    ri = lax.broadcasted_iota(jnp.int32, (blk, blk), 0)
    ci = lax.broadcasted_iota(jnp.int32, (blk, blk), 1)
    causal = ci <= ri
    owns = range(nb)

    q = [q_ref[own * blk:(own + 1) * blk, :] for own in owns]
    s = [_dot_nt(q[own], k_ref[0:(own + 1) * blk, :]) for own in owns]

    sel = {}
    for own in owns:
        if own <= MOBA_TOPK:
            continue
        kmean = kmean_ref[0:own, :]
        km_hi = kmean.astype(BF16)
        km_lo = (kmean - km_hi.astype(F32)).astype(BF16)
        gate = _dot_nt(q[own], km_hi) + _dot_nt(q[own], km_lo)
        lane = lax.broadcasted_iota(jnp.int32, (blk, own), 1)
        rank = jnp.zeros((blk, own), jnp.int32)
        for mth in range(own):
            gm = gate[:, mth:mth + 1]
            beats = (gm > gate) | ((gm == gate) & (lane > mth))
            rank = rank + jnp.where(beats, 1, 0)
        sel[own] = rank < MOBA_TOPK

    pieces = []
    for own in owns:
        past = [s[own][:, n * blk:(n + 1) * blk] for n in range(own)]
        if own in sel:
            past = [jnp.where(sel[own][:, n:n + 1], past[n], MASKED) for n in range(own)]
        pieces.append(past + [jnp.where(causal, s[own][:, own * blk:], MASKED)])
    m_run = [jnp.max(functools.reduce(jnp.maximum, pieces[own]), axis=-1, keepdims=True)
             for own in owns]
    probs = [[jnp.exp2((pc - m_run[own]) * exp2_scale) for pc in pieces[own]] for own in owns]
    denom = [jnp.sum(functools.reduce(lambda a, b: a + b, probs[own]), axis=-1, keepdims=True)
             for own in owns]
    acc = [_dot(jnp.concatenate([pr.astype(BF16) for pr in probs[own]], axis=1),
                v_ref[0:(own + 1) * blk, :]) for own in owns]
    for own in owns:
        o_ref[own * blk:(own + 1) * blk, :] = (acc[own] / denom[own]).astype(o_ref.dtype)


def _moba(p_qk, p_v, *, batch, seq):
    m = p_qk.shape[0]
    nb = seq // MOBA_BLOCK
    hd = MOBA_HEAD_DIM
    return pl.pallas_call(
        functools.partial(_moba_kernel, nb=nb),
        out_shape=jax.ShapeDtypeStruct((m, MOBA_HEADS * hd), BF16),
        grid=(batch, MOBA_HEADS),
        in_specs=[pl.BlockSpec((seq, hd), lambda b, h: (b, h)),
                  pl.BlockSpec((seq, hd), lambda b, h: (b, MOBA_HEADS + h)),
                  pl.BlockSpec((seq, hd), lambda b, h: (b, h))],
        out_specs=pl.BlockSpec((seq, hd), lambda b, h: (b, h)),
        scratch_shapes=[pltpu.VMEM((nb, hd), F32)],
        compiler_params=_params("parallel", "parallel"),
        name="moba_attention",
    )(p_qk, p_qk, p_v)


def _out_proj_kernel(x_ref, mod_ref, post_ref, yg_ref, ym_ref, wg_ref, wm_ref, o_ref, *, row):
    y = _dot(yg_ref[...], wg_ref[...]) + _dot(ym_ref[...], wm_ref[...])
    ga = mod_ref[0, row + 2:row + 3, :]
    o_ref[...] = x_ref[...] + ga * (_rms(y) * post_ref[...])


def _out_proj(x2, mod, row, post_g, y_gdn, y_moba, w_top, w_bot, *, seq, tm=512):
    m, d = x2.shape
    per_b = seq // tm
    kg, km = y_gdn.shape[1], y_moba.shape[1]
    return pl.pallas_call(
        functools.partial(_out_proj_kernel, row=row),
        out_shape=jax.ShapeDtypeStruct((m, d), F32),
        grid=(m // tm,),
        in_specs=[pl.BlockSpec((tm, d), lambda i: (i, 0)),
                  pl.BlockSpec((1, N_MOD, d), lambda i: (i // per_b, 0, 0)),
                  pl.BlockSpec((1, d), lambda i: (0, 0)),
                  pl.BlockSpec((tm, kg), lambda i: (i, 0)),
                  pl.BlockSpec((tm, km), lambda i: (i, 0)),
                  pl.BlockSpec((kg, d), lambda i: (0, 0)),
                  pl.BlockSpec((km, d), lambda i: (0, 0))],
        out_specs=pl.BlockSpec((tm, d), lambda i: (i, 0)),
        compiler_params=_params("parallel"),
        name="mixer_out_proj",
    )(x2, mod, post_g.reshape(1, d), y_gdn, y_moba, w_top, w_bot)


def _group_lanes(vec):
    groups = vec.reshape(GDN_HEADS // GDN_GROUP, GDN_GROUP)
    return jnp.pad(groups, ((0, 0), (0, LANES - GDN_GROUP))).reshape(1, -1)


def _gate_weights(w_a, w_b):
    d = w_a.shape[0]
    parts = []
    for hg in range(GDN_HEADS // GDN_GROUP):
        hs = slice(hg * GDN_GROUP, (hg + 1) * GDN_GROUP)
        parts += [w_a[:, hs], w_b[:, hs], jnp.zeros((d, LANES - 2 * GDN_GROUP), w_a.dtype)]
    return jnp.concatenate(parts, axis=1)


def kernel(x, c, w_ada, b_ada, ffn1_pre_g, ffn1_post_g, ffn1_w_gate, ffn1_w_up, ffn1_w_down, mix_pre_g, mix_post_g, w_in, gdn_conv_w, gdn_a_log, gdn_dt_bias, gdn_norm_g, w_out, ffn2_pre_g, ffn2_post_g, ffn2_w_gate, ffn2_w_up, ffn2_w_down):
    batch, seq, d = x.shape
    depth = w_ada.shape[0]
    gdn_w = GDN_HEADS * GDN_HEAD_DIM
    moba_w = MOBA_HEADS * MOBA_HEAD_DIM
    x2 = x.reshape(batch * seq, d)
    for l in range(depth):
        mod = _adaln(c, w_ada[l], b_ada[l]).reshape(batch, N_MOD, d)

        x2, h2 = _ffn(x2, mod, 0, ffn1_pre_g[l], ffn1_post_g[l], ffn1_w_gate[l].astype(BF16),
                      ffn1_w_up[l].astype(BF16), ffn1_w_down[l].astype(BF16), seq=seq,
                      next_g=mix_pre_g[l])

        wl = w_in[l]
        o_a = 4 * gdn_w
        o_b = o_a + GDN_HEADS
        o_m = o_b + GDN_HEADS
        w_gdn = wl[:, :o_a].astype(BF16)
        w_ab = _gate_weights(wl[:, o_a:o_b], wl[:, o_b:o_m]).astype(BF16)
        w_qk = wl[:, o_m:o_m + 2 * moba_w].astype(BF16)
        w_v = wl[:, o_m + 2 * moba_w:].astype(BF16)
        p_gdn, p_qk, p_v, p_ab = _in_proj(h2, w_gdn, w_qk, w_v, w_ab, seq=seq)
        y_gdn = _gdn(p_gdn, p_ab, gdn_conv_w[l], _group_lanes(gdn_a_log[l]),
                     _group_lanes(gdn_dt_bias[l]), gdn_norm_g[l], batch=batch, seq=seq)
        y_moba = _moba(p_qk, p_v, batch=batch, seq=seq)
        wo = w_out[l].astype(BF16)
        x2 = _out_proj(x2, mod, 3, mix_post_g[l], y_gdn, y_moba, wo[:gdn_w], wo[gdn_w:], seq=seq)

        x2 = _ffn(x2, mod, 6, ffn2_pre_g[l], ffn2_post_g[l], ffn2_w_gate[l].astype(BF16),
                  ffn2_w_up[l].astype(BF16), ffn2_w_down[l].astype(BF16), seq=seq)
    return x2.reshape(batch, seq, d)
```

```python
import functools

import jax
import jax.numpy as jnp
from jax import lax
from jax.experimental import pallas as pl
from jax.experimental.pallas import tpu as pltpu

F32 = jnp.float32
BF16 = jnp.bfloat16

NORM_EPS = 1e-6
N_MOD = 9
GDN_HEADS = 8
GDN_HEAD_DIM = 128
GDN_CONV = 4
GDN_CHUNK = 128
MOBA_HEADS = 8
MOBA_HEAD_DIM = 128
MOBA_BLOCK = 256
MOBA_TOPK = 3
ROPE_DIM = MOBA_HEAD_DIM // 4
ROPE_THETA = 500000.0

LANES = 128
SUBLANES = 8
VMEM_LIMIT = 56 * 1024 * 1024
MASKED = -1e30


def _dot(a, b):
    return jnp.dot(a, b, preferred_element_type=F32)


def _dot_nt(a, b):
    return lax.dot_general(a, b, (((1,), (1,)), ((), ())), preferred_element_type=F32)


def _dot_tn(a, b):
    return lax.dot_general(a, b, (((0,), (0,)), ((), ())), preferred_element_type=F32)


def _rms(x):
    return x * lax.rsqrt(jnp.mean(x * x, axis=-1, keepdims=True) + NORM_EPS)


def _silu(x):
    return x * jax.nn.sigmoid(x)


def _params(*sem):
    return pltpu.CompilerParams(dimension_semantics=sem, vmem_limit_bytes=VMEM_LIMIT)


def _adaln_kernel(c_ref, w_ref, b_ref, o_ref):
    o_ref[...] = _dot(_silu(c_ref[...]), w_ref[...]) + b_ref[...]


def _adaln(c, w, b, *, tn=1024):
    bsz, d = c.shape
    n = w.shape[1]
    rows = -(-bsz // SUBLANES) * SUBLANES
    c_pad = jnp.pad(c, ((0, rows - bsz), (0, 0)))
    out = pl.pallas_call(
        _adaln_kernel,
        out_shape=jax.ShapeDtypeStruct((rows, n), F32),
        grid=(n // tn,),
        in_specs=[pl.BlockSpec((rows, d), lambda j: (0, 0)),
                  pl.BlockSpec((d, tn), lambda j: (0, j)),
                  pl.BlockSpec((1, tn), lambda j: (0, j))],
        out_specs=pl.BlockSpec((rows, tn), lambda j: (0, j)),
        compiler_params=_params("parallel"),
        name="adaln",
    )(c_pad, w, b.reshape(1, n))
    return out[:bsz]


def _mod_norm(x, mod_ref, g_ref, row):
    sh = mod_ref[0, row:row + 1, :]
    sc = mod_ref[0, row + 1:row + 2, :]
    return (_rms(x) * g_ref[...]) * (1.0 + sc) + sh


def _norm_mod_kernel(x_ref, mod_ref, g_ref, h_ref, *, row):
    h_ref[...] = _mod_norm(x_ref[...], mod_ref, g_ref, row).astype(h_ref.dtype)


def _norm_mod(x2, mod, row, g, *, seq, tm=1024):
    m, d = x2.shape
    per_b = seq // tm
    return pl.pallas_call(
        functools.partial(_norm_mod_kernel, row=row),
        out_shape=jax.ShapeDtypeStruct((m, d), BF16),
        grid=(m // tm,),
        in_specs=[pl.BlockSpec((tm, d), lambda i: (i, 0)),
                  pl.BlockSpec((1, N_MOD, d), lambda i: (i // per_b, 0, 0)),
                  pl.BlockSpec((1, d), lambda i: (0, 0))],
        out_specs=pl.BlockSpec((tm, d), lambda i: (i, 0)),
        compiler_params=_params("parallel"),
        name="norm_mod",
    )(x2, mod, g.reshape(1, d))


def _ffn_up_kernel(h_ref, wg_ref, wu_ref, wd_ref, a_ref, wd16_ref, wg16, wu16):
    @pl.when(pl.program_id(1) == 0)
    def _():
        wg16[...] = wg_ref[...].astype(BF16)
        wu16[...] = wu_ref[...].astype(BF16)
        wd16_ref[...] = wd_ref[...].astype(BF16)

    h = h_ref[...]
    a_ref[...] = (_silu(_dot(h, wg16[...])) * _dot(h, wu16[...])).astype(a_ref.dtype)


def _ffn_down_kernel(*refs, row, emit_next):
    if emit_next:
        x_ref, mod_ref, post_ref, next_ref, a_ref, wd_ref, o_ref, hn_ref = refs
    else:
        x_ref, mod_ref, post_ref, a_ref, wd_ref, o_ref = refs
    k = pl.program_id(1)

    @pl.when(k == 0)
    def _():
        o_ref[...] = jnp.zeros_like(o_ref)

    o_ref[...] += _dot(a_ref[...], wd_ref[...])

    @pl.when(k == pl.num_programs(1) - 1)
    def _():
        ga = mod_ref[0, row + 2:row + 3, :]
        out = x_ref[...] + 0.5 * ga * (_rms(o_ref[...]) * post_ref[...])
        o_ref[...] = out
        if emit_next:
            hn_ref[...] = _mod_norm(out, mod_ref, next_ref, row + 3).astype(hn_ref.dtype)


def _ffn(x2, h, mod, row, post_g, wg, wu, wd, *, seq, next_g=None, tm=1024, tf=512):
    m, d = x2.shape
    dff = wg.shape[1]
    per_b = seq // tm
    n_blocks = dff // tf
    act, wd16 = pl.pallas_call(
        _ffn_up_kernel,
        out_shape=(jax.ShapeDtypeStruct((m, dff), BF16), jax.ShapeDtypeStruct((dff, d), BF16)),
        grid=(n_blocks, m // tm),
        in_specs=[pl.BlockSpec((tm, d), lambda j, i: (i, 0)),
                  pl.BlockSpec((d, tf), lambda j, i: (0, j)),
                  pl.BlockSpec((d, tf), lambda j, i: (0, j)),
                  pl.BlockSpec((tf, d), lambda j, i: (j, 0))],
        out_specs=(pl.BlockSpec((tm, tf), lambda j, i: (i, j)),
                   pl.BlockSpec((tf, d), lambda j, i: (j, 0))),
        scratch_shapes=[pltpu.VMEM((d, tf), BF16), pltpu.VMEM((d, tf), BF16)],
        compiler_params=_params("parallel", "arbitrary"),
        name="swiglu_up",
    )(h, wg, wu, wd)

    emit_next = next_g is not None
    row_spec = pl.BlockSpec((1, d), lambda i, k: (0, 0))
    tile_spec = pl.BlockSpec((tm, d), lambda i, k: (i, 0))
    gains = [post_g.reshape(1, d)] + ([next_g.reshape(1, d)] if emit_next else [])
    out_shape = [jax.ShapeDtypeStruct((m, d), F32)] + ([jax.ShapeDtypeStruct((m, d), BF16)] if emit_next else [])
    outs = pl.pallas_call(
        functools.partial(_ffn_down_kernel, row=row, emit_next=emit_next),
        out_shape=out_shape,
        grid=(m // tm, n_blocks),
        in_specs=[tile_spec, pl.BlockSpec((1, N_MOD, d), lambda i, k: (i // per_b, 0, 0))]
                 + [row_spec] * len(gains)
                 + [pl.BlockSpec((tm, tf), lambda i, k: (i, k)),
                    pl.BlockSpec((tf, d), lambda i, k: (k, 0))],
        out_specs=[tile_spec] * len(out_shape),
        compiler_params=_params("parallel", "arbitrary"),
        name="swiglu_down",
    )(x2, mod, *gains, act, wd16)
    return outs if emit_next else outs[0]


def _proj_gdn_kernel(h_ref, w_ref, o_ref, w16):
    @pl.when(pl.program_id(1) == 0)
    def _():
        w16[...] = w_ref[...].astype(BF16)

    o_ref[...] = _dot(h_ref[...], w16[...])


def _proj_qk_kernel(h_ref, w_ref, cos_ref, sa_ref, sb_ref, o_ref, w16):
    @pl.when(pl.program_id(1) == 0)
    def _():
        w16[...] = w_ref[...].astype(BF16)

    y = _dot(h_ref[...], w16[...])
    cos, sa, sb = cos_ref[...], sa_ref[...], sb_ref[...]
    half = ROPE_DIM // 2
    for hd in range(y.shape[1] // LANES):
        yh = y[:, hd * LANES:(hd + 1) * LANES]
        rot = (yh * cos + pltpu.roll(yh, LANES - half, 1) * sa + pltpu.roll(yh, half, 1) * sb)
        o_ref[:, hd * LANES:(hd + 1) * LANES] = rot.astype(o_ref.dtype)


def _proj_v_kernel(h_ref, wv_ref, wab_ref, v_ref, ab_ref, wv16):
    @pl.when(pl.program_id(0) == 0)
    def _():
        wv16[...] = wv_ref[...].astype(BF16)

    h = h_ref[...]
    v_ref[...] = _dot(h, wv16[...]).astype(v_ref.dtype)
    ab_ref[...] = _dot(h, wab_ref[...])


def _rope_tables(seq):
    half = ROPE_DIM // 2
    inv_freq = ROPE_THETA ** (-jnp.arange(0, ROPE_DIM, 2, dtype=F32) / ROPE_DIM)
    ang = jnp.arange(seq).astype(F32)[:, None] * inv_freq[None, :]
    cos, sin = jnp.cos(ang), jnp.sin(ang)
    rest = LANES - 2 * half
    zeros = jnp.zeros((seq, half), F32)
    tail0 = jnp.zeros((seq, rest), F32)
    cos_t = jnp.concatenate([cos, cos, jnp.ones((seq, rest), F32)], axis=1)
    sa_t = jnp.concatenate([-sin, zeros, tail0], axis=1)
    sb_t = jnp.concatenate([zeros, sin, tail0], axis=1)
    return cos_t, sa_t, sb_t


def _in_proj(h2, w_in, w_moba, w_ab, *, seq, tm=1024, tn=1024):
    m, d = h2.shape
    per_b = seq // tm
    gdn_w = GDN_HEADS * GDN_HEAD_DIM
    moba_w = MOBA_HEADS * MOBA_HEAD_DIM
    h_spec = pl.BlockSpec((tm, d), lambda j, i: (i, 0))
    w_spec = pl.BlockSpec((d, tn), lambda j, i: (0, j))
    o_spec = pl.BlockSpec((tm, tn), lambda j, i: (i, j))
    w16 = pltpu.VMEM((d, tn), BF16)

    n_gdn = 4 * gdn_w
    p_gdn = pl.pallas_call(
        _proj_gdn_kernel,
        out_shape=jax.ShapeDtypeStruct((m, n_gdn), F32),
        grid=(n_gdn // tn, m // tm),
        in_specs=[h_spec, w_spec],
        out_specs=o_spec,
        scratch_shapes=[w16],
        compiler_params=_params("parallel", "arbitrary"),
        name="proj_gdn",
    )(h2, w_in)

    n_qk = 2 * moba_w
    cos_t, sa_t, sb_t = _rope_tables(seq)
    tab_spec = pl.BlockSpec((tm, LANES), lambda j, i: (i % per_b, 0))
    p_qk = pl.pallas_call(
        _proj_qk_kernel,
        out_shape=jax.ShapeDtypeStruct((m, n_qk), BF16),
        grid=(n_qk // tn, m // tm),
        in_specs=[h_spec, w_spec, tab_spec, tab_spec, tab_spec],
        out_specs=o_spec,
        scratch_shapes=[w16],
        compiler_params=_params("parallel", "arbitrary"),
        name="proj_moba_qk",
    )(h2, w_moba, cos_t, sa_t, sb_t)

    n_ab = w_ab.shape[1]
    p_v, p_ab = pl.pallas_call(
        _proj_v_kernel,
        out_shape=(jax.ShapeDtypeStruct((m, moba_w), BF16), jax.ShapeDtypeStruct((m, n_ab), F32)),
        grid=(m // tm,),
        in_specs=[pl.BlockSpec((tm, d), lambda i: (i, 0)),
                  pl.BlockSpec((d, moba_w), lambda i: (0, n_qk // moba_w),
                               pipeline_mode=pl.Buffered(1)),
                  pl.BlockSpec((d, n_ab), lambda i: (0, 0))],
        out_specs=(pl.BlockSpec((tm, moba_w), lambda i: (i, 0)),
                   pl.BlockSpec((tm, n_ab), lambda i: (i, 0))),
        scratch_shapes=[pltpu.VMEM((d, moba_w), BF16)],
        compiler_params=_params("arbitrary"),
        name="proj_moba_v_gates",
    )(h2, w_moba, w_ab)
    return p_gdn, p_qk, p_v, p_ab


GDN_GROUP = 4
GDN_HALO = SUBLANES


def _gdn_kernel(q_ref, k_ref, v_ref, z_ref, ab_ref, cwq_ref, cwk_ref, cwv_ref, alog_ref, dtb_ref,
                ng_ref, o_ref, xq, xk, xv, state_ref, oacc, *, tt):
    t = pl.program_id(2)
    c = GDN_CHUNK
    hd = GDN_HEAD_DIM
    halo = GDN_HALO

    @pl.when(t == 0)
    def _():
        for buf in (xq, xk, xv):
            buf[0:halo, :] = jnp.zeros((halo, buf.shape[1]), F32)
        state_ref[...] = jnp.zeros_like(state_ref)

    @pl.when(t > 0)
    def _():
        for buf in (xq, xk, xv):
            buf[0:halo, :] = buf[tt:tt + halo, :]

    xq[halo:halo + tt, :] = q_ref[...]
    xk[halo:halo + tt, :] = k_ref[...]
    xv[halo:halo + tt, :] = v_ref[...]

    def conv_silu(buf, cw_ref):
        base = halo - (GDN_CONV - 1)
        acc = buf[pl.ds(base, tt), :] * cw_ref[0:1, :]
        for j in range(1, GDN_CONV):
            acc = acc + buf[pl.ds(base + j, tt), :] * cw_ref[j:j + 1, :]
        return _silu(acc)

    qc = conv_silu(xq, cwq_ref)
    kc = conv_silu(xk, cwk_ref)
    vc = conv_silu(xv, cwv_ref)

    ab = ab_ref[...]
    pre = ab + dtb_ref[...]
    softplus = jnp.maximum(pre, 0.0) + jnp.log1p(jnp.exp(-jnp.abs(pre)))
    g = -jnp.exp(alog_ref[...]) * softplus
    beta = jax.nn.sigmoid(ab)
    rowc = lax.broadcasted_iota(jnp.int32, (tt, LANES), 0) & (c - 1)
    gcum = g
    shift = 1
    while shift < c:
        gcum = gcum + jnp.where(rowc >= shift, pltpu.roll(gcum, shift, 0), 0.0)
        shift *= 2
    gcum_t = gcum.T
    beta_t = beta.T

    ri = lax.broadcasted_iota(jnp.int32, (c, c), 0)
    ci = lax.broadcasted_iota(jnp.int32, (c, c), 1)
    tril = ri >= ci
    strict = ri > ci
    eye = (ri == ci).astype(F32)

    n_ch = tt // c
    heads = range(GDN_GROUP)
    items = [(hh, ch) for hh in heads for ch in range(n_ch)]
    qn, kn, vn = [], [], []
    for hh in heads:
        ls = slice(hh * hd, (hh + 1) * hd)
        qh, kh = qc[:, ls], kc[:, ls]
        qn.append(((qh * lax.rsqrt(jnp.sum(qh * qh, axis=-1, keepdims=True) + NORM_EPS))
                   * (hd ** -0.5)).astype(BF16))
        kn.append((kh * lax.rsqrt(jnp.sum(kh * kh, axis=-1, keepdims=True) + NORM_EPS)).astype(BF16))
        vn.append(vc[:, ls].astype(BF16))

    def rows(ch):
        return slice(ch * c, (ch + 1) * c)

    kb = {(hh, ch): kn[hh][rows(ch)] for hh, ch in items}
    qb = {(hh, ch): qn[hh][rows(ch)] for hh, ch in items}
    vb = {(hh, ch): vn[hh][rows(ch)] for hh, ch in items}
    g_col = {(hh, ch): gcum[rows(ch), hh:hh + 1] for hh, ch in items}
    g_row = {(hh, ch): gcum_t[hh:hh + 1, rows(ch)] for hh, ch in items}
    b_col = {(hh, ch): beta[rows(ch), GDN_GROUP + hh:GDN_GROUP + hh + 1] for hh, ch in items}
    b_row = {(hh, ch): beta_t[GDN_GROUP + hh:GDN_GROUP + hh + 1, rows(ch)] for hh, ch in items}
    g_last = {(hh, ch): gcum[ch * c + c - 1:ch * c + c, hh:hh + 1] for hh, ch in items}

    decay = {it: jnp.exp(jnp.where(tril, g_col[it] - g_row[it], MASKED)) for it in items}
    kk = {it: _dot_nt(kb[it], kb[it]) for it in items}
    qk = {it: _dot_nt(qb[it], kb[it]) for it in items}
    low = {it: jnp.where(strict, b_col[it] * kk[it] * decay[it], 0.0) for it in items}
    attn = {it: jnp.where(tril, qk[it] * decay[it], 0.0).astype(BF16) for it in items}

    inv = {it: eye - jnp.where((ri >> 1) == (ci >> 1), low[it], 0.0) for it in items}
    size = 4
    while size <= c:
        bits = size.bit_length() - 1
        off = ((ri >> bits) == (ci >> bits)) & ((ri >> (bits - 1)) != (ci >> (bits - 1)))
        ib = {it: inv[it].astype(BF16) for it in items}
        right = {it: _dot(jnp.where(off, low[it], 0.0).astype(BF16), ib[it]).astype(BF16)
                 for it in items}
        inv = {it: inv[it] - _dot(ib[it], right[it]) for it in items}
        size *= 2

    tb = {it: inv[it] * b_row[it] for it in items}
    u = {it: _dot(tb[it].astype(BF16), vb[it]) for it in items}
    w = {it: _dot((tb[it] * jnp.exp(g_row[it])).astype(BF16), kb[it]).astype(BF16) for it in items}
    e_col = {it: jnp.exp(g_col[it]) for it in items}
    e_dec = {it: jnp.exp(g_last[it] - g_col[it]) for it in items}
    e_last = {it: jnp.exp(g_last[it]) for it in items}

    state = [state_ref[hh] for hh in heads]
    for ch in range(n_ch):
        sb = [state[hh].astype(BF16) for hh in heads]
        w_s = [_dot(w[(hh, ch)], sb[hh]) for hh in heads]
        q_s = [_dot(qb[(hh, ch)], sb[hh]) for hh in heads]
        v_new = [u[(hh, ch)] - w_s[hh] for hh in heads]
        v_dec = [(v_new[hh] * e_dec[(hh, ch)]).astype(BF16) for hh in heads]
        upd = [_dot_tn(kb[(hh, ch)], v_dec[hh]) for hh in heads]
        intra = [_dot(attn[(hh, ch)], v_new[hh].astype(BF16)) for hh in heads]
        for hh in heads:
            oacc[rows(ch), hh * hd:(hh + 1) * hd] = e_col[(hh, ch)] * q_s[hh] + intra[hh]
        state = [state[hh] * e_last[(hh, ch)] + upd[hh] for hh in heads]
    for hh in heads:
        state_ref[hh] = state[hh]

    z = z_ref[...]
    ng = ng_ref[...]
    for hh in range(GDN_GROUP):
        ls = slice(hh * hd, (hh + 1) * hd)
        o_ref[:, ls] = ((_rms(oacc[:, ls]) * ng) * _silu(z[:, ls])).astype(o_ref.dtype)


def _gdn(p_gdn, p_ab, conv_w, alog2, dtb2, norm_g, *, batch, seq, tt=512):
    m = p_gdn.shape[0]
    gw = GDN_GROUP * GDN_HEAD_DIM
    n_groups = GDN_HEADS // GDN_GROUP
    nt = seq // tt

    def col(off):
        return pl.BlockSpec((tt, gw), lambda b, hg, t: (b * nt + t, off + hg))

    def cw(off):
        return pl.BlockSpec((GDN_CONV, gw), lambda b, hg, t: (0, off + hg))

    lane_spec = pl.BlockSpec((1, LANES), lambda b, hg, t: (0, hg))
    return pl.pallas_call(
        functools.partial(_gdn_kernel, tt=tt),
        out_shape=jax.ShapeDtypeStruct((m, GDN_HEADS * GDN_HEAD_DIM), BF16),
        grid=(batch, n_groups, nt),
        in_specs=[col(0), col(n_groups), col(2 * n_groups), col(3 * n_groups),
                  pl.BlockSpec((tt, LANES), lambda b, hg, t: (b * nt + t, hg)),
                  cw(0), cw(n_groups), cw(2 * n_groups),
                  lane_spec, lane_spec,
                  pl.BlockSpec((1, GDN_HEAD_DIM), lambda b, hg, t: (0, 0))],
        out_specs=pl.BlockSpec((tt, gw), lambda b, hg, t: (b * nt + t, hg)),
        scratch_shapes=[pltpu.VMEM((tt + GDN_HALO, gw), F32),
                        pltpu.VMEM((tt + GDN_HALO, gw), F32),
                        pltpu.VMEM((tt + GDN_HALO, gw), F32),
                        pltpu.VMEM((GDN_GROUP, GDN_HEAD_DIM, GDN_HEAD_DIM), F32),
                        pltpu.VMEM((tt, gw), F32)],
        compiler_params=_params("parallel", "parallel", "arbitrary"),
        name="gated_deltanet",
    )(p_gdn, p_gdn, p_gdn, p_gdn, p_ab, conv_w, conv_w, conv_w, alog2, dtb2,
      norm_g.reshape(1, GDN_HEAD_DIM))


def _moba_kernel(q_ref, k_ref, v_ref, o_ref, kmean_ref, *, nb):
    blk = MOBA_BLOCK
    exp2_scale = (MOBA_HEAD_DIM ** -0.5) * 1.4426950408889634

    for n in range(nb - 1):
        kblk = k_ref[n * blk:(n + 1) * blk, :].astype(F32)
        kmean_ref[n:n + 1, :] = jnp.sum(kblk, axis=0, keepdims=True) * (1.0 / blk)

    ri = lax.broadcasted_iota(jnp.int32, (blk, blk), 0)
    ci = lax.broadcasted_iota(jnp.int32, (blk, blk), 1)
    causal = ci <= ri
    owns = range(nb)

    q = [q_ref[own * blk:(own + 1) * blk, :] for own in owns]
    s = [_dot_nt(q[own], k_ref[0:(own + 1) * blk, :]) for own in owns]

    sel = {}
    for own in owns:
        if own <= MOBA_TOPK:
            continue
        kmean = kmean_ref[0:own, :]
        km_hi = kmean.astype(BF16)
        km_lo = (kmean - km_hi.astype(F32)).astype(BF16)
        gate = _dot_nt(q[own], km_hi) + _dot_nt(q[own], km_lo)
        lane = lax.broadcasted_iota(jnp.int32, (blk, own), 1)
        rank = jnp.zeros((blk, own), jnp.int32)
        for mth in range(own):
            gm = gate[:, mth:mth + 1]
            beats = (gm > gate) | ((gm == gate) & (lane > mth))
            rank = rank + jnp.where(beats, 1, 0)
        sel[own] = rank < MOBA_TOPK

    pieces = []
    for own in owns:
        past = [s[own][:, n * blk:(n + 1) * blk] for n in range(own)]
        if own in sel:
            past = [jnp.where(sel[own][:, n:n + 1], past[n], MASKED) for n in range(own)]
        pieces.append(past + [jnp.where(causal, s[own][:, own * blk:], MASKED)])
    m_run = [jnp.max(functools.reduce(jnp.maximum, pieces[own]), axis=-1, keepdims=True)
             for own in owns]
    probs = [[jnp.exp2((pc - m_run[own]) * exp2_scale) for pc in pieces[own]] for own in owns]
    denom = [jnp.sum(functools.reduce(lambda a, b: a + b, probs[own]), axis=-1, keepdims=True)
             for own in owns]
    acc = [_dot(jnp.concatenate([pr.astype(BF16) for pr in probs[own]], axis=1),
                v_ref[0:(own + 1) * blk, :]) for own in owns]
    for own in owns:
        o_ref[own * blk:(own + 1) * blk, :] = (acc[own] / denom[own]).astype(o_ref.dtype)


def _moba(p_qk, p_v, *, batch, seq):
    m = p_qk.shape[0]
    nb = seq // MOBA_BLOCK
    hd = MOBA_HEAD_DIM
    return pl.pallas_call(
        functools.partial(_moba_kernel, nb=nb),
        out_shape=jax.ShapeDtypeStruct((m, MOBA_HEADS * hd), BF16),
        grid=(batch, MOBA_HEADS),
        in_specs=[pl.BlockSpec((seq, hd), lambda b, h: (b, h)),
                  pl.BlockSpec((seq, hd), lambda b, h: (b, MOBA_HEADS + h)),
                  pl.BlockSpec((seq, hd), lambda b, h: (b, h))],
        out_specs=pl.BlockSpec((seq, hd), lambda b, h: (b, h)),
        scratch_shapes=[pltpu.VMEM((nb, hd), F32)],
        compiler_params=_params("parallel", "parallel"),
        name="moba_attention",
    )(p_qk, p_qk, p_v)


def _out_proj_kernel(x_ref, mod_ref, post_ref, next_ref, yg_ref, ym_ref, w_ref, o_ref, hn_ref, w16,
                     *, row):
    @pl.when(pl.program_id(0) == 0)
    def _():
        w16[...] = w_ref[...].astype(BF16)

    kg = yg_ref.shape[1]
    y = _dot(yg_ref[...], w16[0:kg, :]) + _dot(ym_ref[...], w16[kg:, :])
    ga = mod_ref[0, row + 2:row + 3, :]
    out = x_ref[...] + ga * (_rms(y) * post_ref[...])
    o_ref[...] = out
    hn_ref[...] = _mod_norm(out, mod_ref, next_ref, row + 3).astype(hn_ref.dtype)


def _out_proj(x2, mod, row, post_g, next_g, y_gdn, y_moba, w_out, *, seq, tm=512):
    m, d = x2.shape
    per_b = seq // tm
    kg, km = y_gdn.shape[1], y_moba.shape[1]
    tile = pl.BlockSpec((tm, d), lambda i: (i, 0))
    row_spec = pl.BlockSpec((1, d), lambda i: (0, 0))
    return pl.pallas_call(
        functools.partial(_out_proj_kernel, row=row),
        out_shape=(jax.ShapeDtypeStruct((m, d), F32), jax.ShapeDtypeStruct((m, d), BF16)),
        grid=(m // tm,),
        in_specs=[tile,
                  pl.BlockSpec((1, N_MOD, d), lambda i: (i // per_b, 0, 0)),
                  row_spec, row_spec,
                  pl.BlockSpec((tm, kg), lambda i: (i, 0)),
                  pl.BlockSpec((tm, km), lambda i: (i, 0)),
                  pl.BlockSpec((kg + km, d), lambda i: (0, 0), pipeline_mode=pl.Buffered(1))],
        out_specs=(tile, tile),
        scratch_shapes=[pltpu.VMEM((kg + km, d), BF16)],
        compiler_params=_params("arbitrary"),
        name="mixer_out_proj",
    )(x2, mod, post_g.reshape(1, d), next_g.reshape(1, d), y_gdn, y_moba, w_out)


def _group_lanes(vec):
    groups = vec.reshape(GDN_HEADS // GDN_GROUP, GDN_GROUP)
    return jnp.pad(groups, ((0, 0), (0, LANES - GDN_GROUP))).reshape(1, -1)


def _gate_weights(w_a, w_b):
    d = w_a.shape[0]
    parts = []
    for hg in range(GDN_HEADS // GDN_GROUP):
        hs = slice(hg * GDN_GROUP, (hg + 1) * GDN_GROUP)
        parts += [w_a[:, hs], w_b[:, hs], jnp.zeros((d, LANES - 2 * GDN_GROUP), w_a.dtype)]
    return jnp.concatenate(parts, axis=1)


def kernel(x, c, w_ada, b_ada, ffn1_pre_g, ffn1_post_g, ffn1_w_gate, ffn1_w_up, ffn1_w_down, mix_pre_g, mix_post_g, w_in, gdn_conv_w, gdn_a_log, gdn_dt_bias, gdn_norm_g, w_out, ffn2_pre_g, ffn2_post_g, ffn2_w_gate, ffn2_w_up, ffn2_w_down):
    batch, seq, d = x.shape
    depth = w_ada.shape[0]
    gdn_w = GDN_HEADS * GDN_HEAD_DIM
    x2 = x.reshape(batch * seq, d)
    for l in range(depth):
        mod = _adaln(c, w_ada[l], b_ada[l]).reshape(batch, N_MOD, d)

        h1 = _norm_mod(x2, mod, 0, ffn1_pre_g[l], seq=seq)
        x2, h2 = _ffn(x2, h1, mod, 0, ffn1_post_g[l], ffn1_w_gate[l], ffn1_w_up[l], ffn1_w_down[l],
                      seq=seq, next_g=mix_pre_g[l])

        wl = w_in[l]
        o_a = 4 * gdn_w
        o_b = o_a + GDN_HEADS
        o_m = o_b + GDN_HEADS
        w_ab = _gate_weights(wl[:, o_a:o_b], wl[:, o_b:o_m]).astype(BF16)
        p_gdn, p_qk, p_v, p_ab = _in_proj(h2, wl, wl[:, o_m:], w_ab, seq=seq)
        y_gdn = _gdn(p_gdn, p_ab, gdn_conv_w[l], _group_lanes(gdn_a_log[l]),
                     _group_lanes(gdn_dt_bias[l]), gdn_norm_g[l], batch=batch, seq=seq)
        y_moba = _moba(p_qk, p_v, batch=batch, seq=seq)
        x2, h3 = _out_proj(x2, mod, 3, mix_post_g[l], ffn2_pre_g[l], y_gdn, y_moba, w_out[l], seq=seq)

        x2 = _ffn(x2, h3, mod, 6, ffn2_post_g[l], ffn2_w_gate[l], ffn2_w_up[l], ffn2_w_down[l],
                  seq=seq)
    return x2.reshape(batch, seq, d)
```

```python
import functools

import jax
import jax.numpy as jnp
from jax import lax
from jax.experimental import pallas as pl
from jax.experimental.pallas import tpu as pltpu

F32 = jnp.float32
BF16 = jnp.bfloat16

NORM_EPS = 1e-6
N_MOD = 9
GDN_HEADS = 8
GDN_HEAD_DIM = 128
GDN_CONV = 4
GDN_CHUNK = 128
MOBA_HEADS = 8
MOBA_HEAD_DIM = 128
MOBA_BLOCK = 256
MOBA_TOPK = 3
ROPE_DIM = MOBA_HEAD_DIM // 4
ROPE_THETA = 500000.0

LANES = 128
SUBLANES = 8
VMEM_LIMIT = 56 * 1024 * 1024
MASKED = -1e30


def _dot(a, b):
    return jnp.dot(a, b, preferred_element_type=F32)


def _dot_nt(a, b):
    return lax.dot_general(a, b, (((1,), (1,)), ((), ())), preferred_element_type=F32)


def _dot_tn(a, b):
    return lax.dot_general(a, b, (((0,), (0,)), ((), ())), preferred_element_type=F32)


def _rms(x):
    return x * lax.rsqrt(jnp.mean(x * x, axis=-1, keepdims=True) + NORM_EPS)


def _silu(x):
    return x * jax.nn.sigmoid(x)


def _params(*sem):
    return pltpu.CompilerParams(dimension_semantics=sem, vmem_limit_bytes=VMEM_LIMIT)


def _adaln_kernel(c_ref, w_ref, b_ref, o_ref):
    o_ref[...] = _dot(_silu(c_ref[...]), w_ref[...]) + b_ref[...]


def _adaln(c, w, b, *, tn=1024):
    bsz, d = c.shape
    n = w.shape[1]
    rows = -(-bsz // SUBLANES) * SUBLANES
    c_pad = jnp.pad(c, ((0, rows - bsz), (0, 0)))
    out = pl.pallas_call(
        _adaln_kernel,
        out_shape=jax.ShapeDtypeStruct((rows, n), F32),
        grid=(n // tn,),
        in_specs=[pl.BlockSpec((rows, d), lambda j: (0, 0)),
                  pl.BlockSpec((d, tn), lambda j: (0, j)),
                  pl.BlockSpec((1, tn), lambda j: (0, j))],
        out_specs=pl.BlockSpec((rows, tn), lambda j: (0, j)),
        compiler_params=_params("parallel"),
        name="adaln",
    )(c_pad, w, b.reshape(1, n))
    return out[:bsz]


def _mod_norm(x, mod_ref, g_ref, row):
    sh = mod_ref[0, row:row + 1, :]
    sc = mod_ref[0, row + 1:row + 2, :]
    return (_rms(x) * g_ref[...]) * (1.0 + sc) + sh


def _norm_mod_kernel(x_ref, mod_ref, g_ref, h_ref, *, row):
    h_ref[...] = _mod_norm(x_ref[...], mod_ref, g_ref, row).astype(h_ref.dtype)


def _norm_mod(x2, mod, row, g, *, seq, tm=1024):
    m, d = x2.shape
    per_b = seq // tm
    return pl.pallas_call(
        functools.partial(_norm_mod_kernel, row=row),
        out_shape=jax.ShapeDtypeStruct((m, d), BF16),
        grid=(m // tm,),
        in_specs=[pl.BlockSpec((tm, d), lambda i: (i, 0)),
                  pl.BlockSpec((1, N_MOD, d), lambda i: (i // per_b, 0, 0)),
                  pl.BlockSpec((1, d), lambda i: (0, 0))],
        out_specs=pl.BlockSpec((tm, d), lambda i: (i, 0)),
        compiler_params=_params("parallel"),
        name="norm_mod",
    )(x2, mod, g.reshape(1, d))


def _ffn_up_kernel(h_ref, wg_ref, wu_ref, wd_ref, a_ref, wd16_ref, wg16, wu16):
    @pl.when(pl.program_id(1) == 0)
    def _():
        wg16[...] = wg_ref[...].astype(BF16)
        wu16[...] = wu_ref[...].astype(BF16)
        wd16_ref[...] = wd_ref[...].astype(BF16)

    h = h_ref[...]
    a_ref[...] = (_silu(_dot(h, wg16[...])) * _dot(h, wu16[...])).astype(a_ref.dtype)


def _ffn_down_kernel(*refs, row, emit_next):
    if emit_next:
        x_ref, mod_ref, post_ref, next_ref, a_ref, wd_ref, o_ref, hn_ref = refs
    else:
        x_ref, mod_ref, post_ref, a_ref, wd_ref, o_ref = refs
    k = pl.program_id(1)

    @pl.when(k == 0)
    def _():
        o_ref[...] = jnp.zeros_like(o_ref)

    o_ref[...] += _dot(a_ref[...], wd_ref[...])

    @pl.when(k == pl.num_programs(1) - 1)
    def _():
        ga = mod_ref[0, row + 2:row + 3, :]
        out = x_ref[...] + 0.5 * ga * (_rms(o_ref[...]) * post_ref[...])
        o_ref[...] = out
        if emit_next:
            hn_ref[...] = _mod_norm(out, mod_ref, next_ref, row + 3).astype(hn_ref.dtype)


def _ffn(x2, h, mod, row, post_g, wg, wu, wd, *, seq, next_g=None, tm_up=1024, tf=512, tm_down=512,
         k_steps=2):
    m, d = x2.shape
    dff = wg.shape[1]
    act, wd16 = pl.pallas_call(
        _ffn_up_kernel,
        out_shape=(jax.ShapeDtypeStruct((m, dff), BF16), jax.ShapeDtypeStruct((dff, d), BF16)),
        grid=(dff // tf, m // tm_up),
        in_specs=[pl.BlockSpec((tm_up, d), lambda j, i: (i, 0)),
                  pl.BlockSpec((d, tf), lambda j, i: (0, j)),
                  pl.BlockSpec((d, tf), lambda j, i: (0, j)),
                  pl.BlockSpec((tf, d), lambda j, i: (j, 0))],
        out_specs=(pl.BlockSpec((tm_up, tf), lambda j, i: (i, j)),
                   pl.BlockSpec((tf, d), lambda j, i: (j, 0))),
        scratch_shapes=[pltpu.VMEM((d, tf), BF16), pltpu.VMEM((d, tf), BF16)],
        compiler_params=_params("parallel", "arbitrary"),
        name="swiglu_up",
    )(h, wg, wu, wd)

    tm = tm_down
    tk = dff // k_steps
    per_b = seq // tm
    emit_next = next_g is not None
    row_spec = pl.BlockSpec((1, d), lambda i, k: (0, 0))
    tile_spec = pl.BlockSpec((tm, d), lambda i, k: (i, 0))
    gains = [post_g.reshape(1, d)] + ([next_g.reshape(1, d)] if emit_next else [])
    out_shape = [jax.ShapeDtypeStruct((m, d), F32)] + ([jax.ShapeDtypeStruct((m, d), BF16)] if emit_next else [])
    outs = pl.pallas_call(
        functools.partial(_ffn_down_kernel, row=row, emit_next=emit_next),
        out_shape=out_shape,
        grid=(m // tm, k_steps),
        in_specs=[tile_spec, pl.BlockSpec((1, N_MOD, d), lambda i, k: (i // per_b, 0, 0))]
                 + [row_spec] * len(gains)
                 + [pl.BlockSpec((tm, tk), lambda i, k: (i, k)),
                    pl.BlockSpec((tk, d), lambda i, k: (k, 0))],
        out_specs=[tile_spec] * len(out_shape),
        compiler_params=_params("parallel", "arbitrary"),
        name="swiglu_down",
    )(x2, mod, *gains, act, wd16)
    return outs if emit_next else outs[0]


def _proj_gdn_kernel(h_ref, w_ref, o_ref, w16):
    @pl.when(pl.program_id(1) == 0)
    def _():
        w16[...] = w_ref[...].astype(BF16)

    o_ref[...] = _dot_nt(h_ref[...], w16[...])


def _proj_qk_kernel(h_ref, w_ref, cos_ref, sa_ref, sb_ref, o_ref, w16):
    @pl.when(pl.program_id(1) == 0)
    def _():
        w16[...] = w_ref[...].astype(BF16)

    y = _dot_nt(h_ref[...], w16[...])
    cos, sa, sb = cos_ref[...], sa_ref[...], sb_ref[...]
    half = ROPE_DIM // 2
    for hd in range(y.shape[1] // LANES):
        yh = y[:, hd * LANES:(hd + 1) * LANES]
        rot = (yh * cos + pltpu.roll(yh, LANES - half, 1) * sa + pltpu.roll(yh, half, 1) * sb)
        o_ref[:, hd * LANES:(hd + 1) * LANES] = rot.astype(o_ref.dtype)


def _proj_v_kernel(h_ref, wv_ref, wab_ref, v_ref, ab_ref, wv16):
    @pl.when(pl.program_id(0) == 0)
    def _():
        wv16[...] = wv_ref[...].astype(BF16)

    h = h_ref[...]
    v_ref[...] = _dot_nt(h, wv16[...]).astype(v_ref.dtype)
    ab_ref[...] = _dot_nt(h, wab_ref[...].astype(BF16))


def _rope_tables(seq):
    half = ROPE_DIM // 2
    inv_freq = ROPE_THETA ** (-jnp.arange(0, ROPE_DIM, 2, dtype=F32) / ROPE_DIM)
    ang = jnp.arange(seq).astype(F32)[:, None] * inv_freq[None, :]
    cos, sin = jnp.cos(ang), jnp.sin(ang)
    rest = LANES - 2 * half
    zeros = jnp.zeros((seq, half), F32)
    tail0 = jnp.zeros((seq, rest), F32)
    cos_t = jnp.concatenate([cos, cos, jnp.ones((seq, rest), F32)], axis=1)
    sa_t = jnp.concatenate([-sin, zeros, tail0], axis=1)
    sb_t = jnp.concatenate([zeros, sin, tail0], axis=1)
    return cos_t, sa_t, sb_t


def _in_proj(h2, w_t, w_ab_t, *, seq, tm=1024, tn=1024):
    m, d = h2.shape
    per_b = seq // tm
    gdn_w = GDN_HEADS * GDN_HEAD_DIM
    moba_w = MOBA_HEADS * MOBA_HEAD_DIM
    n_gdn = 4 * gdn_w
    moba_row0 = n_gdn + 2 * GDN_HEADS
    h_spec = pl.BlockSpec((tm, d), lambda j, i: (i, 0))
    o_spec = pl.BlockSpec((tm, tn), lambda j, i: (i, j))
    w16 = pltpu.VMEM((tn, d), BF16)

    p_gdn = pl.pallas_call(
        _proj_gdn_kernel,
        out_shape=jax.ShapeDtypeStruct((m, n_gdn), F32),
        grid=(n_gdn // tn, m // tm),
        in_specs=[h_spec, pl.BlockSpec((tn, d), lambda j, i: (j, 0))],
        out_specs=o_spec,
        scratch_shapes=[w16],
        compiler_params=_params("parallel", "arbitrary"),
        name="proj_gdn",
    )(h2, w_t)

    n_qk = 2 * moba_w
    cos_t, sa_t, sb_t = _rope_tables(seq)
    tab_spec = pl.BlockSpec((tm, LANES), lambda j, i: (i % per_b, 0))
    p_qk = pl.pallas_call(
        _proj_qk_kernel,
        out_shape=jax.ShapeDtypeStruct((m, n_qk), BF16),
        grid=(n_qk // tn, m // tm),
        in_specs=[h_spec,
                  pl.BlockSpec((pl.Element(tn), pl.Element(d)),
                               lambda j, i: ((moba_row0 // SUBLANES + j * (tn // SUBLANES)) * SUBLANES, 0)),
                  tab_spec, tab_spec, tab_spec],
        out_specs=o_spec,
        scratch_shapes=[w16],
        compiler_params=_params("parallel", "arbitrary"),
        name="proj_moba_qk",
    )(h2, w_t, cos_t, sa_t, sb_t)

    n_ab = w_ab_t.shape[0]
    p_v, p_ab = pl.pallas_call(
        _proj_v_kernel,
        out_shape=(jax.ShapeDtypeStruct((m, moba_w), BF16), jax.ShapeDtypeStruct((m, n_ab), F32)),
        grid=(m // tm,),
        in_specs=[pl.BlockSpec((tm, d), lambda i: (i, 0)),
                  pl.BlockSpec((pl.Element(moba_w), pl.Element(d)), lambda i: (moba_row0 + n_qk, 0),
                               pipeline_mode=pl.Buffered(1)),
                  pl.BlockSpec((n_ab, d), lambda i: (0, 0))],
        out_specs=(pl.BlockSpec((tm, moba_w), lambda i: (i, 0)),
                   pl.BlockSpec((tm, n_ab), lambda i: (i, 0))),
        scratch_shapes=[pltpu.VMEM((moba_w, d), BF16)],
        compiler_params=_params("arbitrary"),
        name="proj_moba_v_gates",
    )(h2, w_t, w_ab_t)
    return p_gdn, p_qk, p_v, p_ab


GDN_GROUP = 4
GDN_HALO = SUBLANES


def _gdn_kernel(q_ref, k_ref, v_ref, z_ref, ab_ref, cwq_ref, cwk_ref, cwv_ref, alog_ref, dtb_ref,
                ng_ref, o_ref, xq, xk, xv, state_ref, oacc, *, tt):
    t = pl.program_id(2)
    c = GDN_CHUNK
    hd = GDN_HEAD_DIM
    halo = GDN_HALO

    @pl.when(t == 0)
    def _():
        for buf in (xq, xk, xv):
            buf[0:halo, :] = jnp.zeros((halo, buf.shape[1]), F32)
        state_ref[...] = jnp.zeros_like(state_ref)

    @pl.when(t > 0)
    def _():
        for buf in (xq, xk, xv):
            buf[0:halo, :] = buf[tt:tt + halo, :]

    xq[halo:halo + tt, :] = q_ref[...]
    xk[halo:halo + tt, :] = k_ref[...]
    xv[halo:halo + tt, :] = v_ref[...]

    def conv_silu(buf, cw_ref):
        base = halo - (GDN_CONV - 1)
        acc = buf[pl.ds(base, tt), :] * cw_ref[0:1, :]
        for j in range(1, GDN_CONV):
            acc = acc + buf[pl.ds(base + j, tt), :] * cw_ref[j:j + 1, :]
        return _silu(acc)

    qc = conv_silu(xq, cwq_ref)
    kc = conv_silu(xk, cwk_ref)
    vc = conv_silu(xv, cwv_ref)

    ab = ab_ref[...]
    pre = ab + dtb_ref[...]
    softplus = jnp.maximum(pre, 0.0) + jnp.log1p(jnp.exp(-jnp.abs(pre)))
    g = -jnp.exp(alog_ref[...]) * softplus
    beta = jax.nn.sigmoid(ab)
    rowc = lax.broadcasted_iota(jnp.int32, (tt, LANES), 0) & (c - 1)
    gcum = g
    shift = 1
    while shift < c:
        gcum = gcum + jnp.where(rowc >= shift, pltpu.roll(gcum, shift, 0), 0.0)
        shift *= 2
    gcum_t = gcum.T
    beta_t = beta.T

    ri = lax.broadcasted_iota(jnp.int32, (c, c), 0)
    ci = lax.broadcasted_iota(jnp.int32, (c, c), 1)
    tril = ri >= ci
    strict = ri > ci
    eye = (ri == ci).astype(F32)

    n_ch = tt // c
    heads = range(GDN_GROUP)
    items = [(hh, ch) for hh in heads for ch in range(n_ch)]
    qn, kn, vn = [], [], []
    for hh in heads:
        ls = slice(hh * hd, (hh + 1) * hd)
        qh, kh = qc[:, ls], kc[:, ls]
        qn.append(((qh * lax.rsqrt(jnp.sum(qh * qh, axis=-1, keepdims=True) + NORM_EPS))
                   * (hd ** -0.5)).astype(BF16))
        kn.append((kh * lax.rsqrt(jnp.sum(kh * kh, axis=-1, keepdims=True) + NORM_EPS)).astype(BF16))
        vn.append(vc[:, ls].astype(BF16))

    def rows(ch):
        return slice(ch * c, (ch + 1) * c)

    kb = {(hh, ch): kn[hh][rows(ch)] for hh, ch in items}
    qb = {(hh, ch): qn[hh][rows(ch)] for hh, ch in items}
    vb = {(hh, ch): vn[hh][rows(ch)] for hh, ch in items}
    g_col = {(hh, ch): gcum[rows(ch), hh:hh + 1] for hh, ch in items}
    g_row = {(hh, ch): gcum_t[hh:hh + 1, rows(ch)] for hh, ch in items}
    b_col = {(hh, ch): beta[rows(ch), GDN_GROUP + hh:GDN_GROUP + hh + 1] for hh, ch in items}
    b_row = {(hh, ch): beta_t[GDN_GROUP + hh:GDN_GROUP + hh + 1, rows(ch)] for hh, ch in items}
    g_last = {(hh, ch): gcum[ch * c + c - 1:ch * c + c, hh:hh + 1] for hh, ch in items}

    decay = {it: jnp.exp(jnp.where(tril, g_col[it] - g_row[it], MASKED)) for it in items}
    kk = {it: _dot_nt(kb[it], kb[it]) for it in items}
    qk = {it: _dot_nt(qb[it], kb[it]) for it in items}
    low = {it: jnp.where(strict, b_col[it] * kk[it] * decay[it], 0.0) for it in items}
    attn = {it: jnp.where(tril, qk[it] * decay[it], 0.0).astype(BF16) for it in items}

    inv = {it: eye - jnp.where((ri >> 1) == (ci >> 1), low[it], 0.0) for it in items}
    size = 4
    while size <= c:
        bits = size.bit_length() - 1
        off = ((ri >> bits) == (ci >> bits)) & ((ri >> (bits - 1)) != (ci >> (bits - 1)))
        ib = {it: inv[it].astype(BF16) for it in items}
        right = {it: _dot(jnp.where(off, low[it], 0.0).astype(BF16), ib[it]).astype(BF16)
                 for it in items}
        inv = {it: inv[it] - _dot(ib[it], right[it]) for it in items}
        size *= 2

    tb = {it: inv[it] * b_row[it] for it in items}
    u = {it: _dot(tb[it].astype(BF16), vb[it]) for it in items}
    w = {it: _dot((tb[it] * jnp.exp(g_row[it])).astype(BF16), kb[it]).astype(BF16) for it in items}
    e_col = {it: jnp.exp(g_col[it]) for it in items}
    e_dec = {it: jnp.exp(g_last[it] - g_col[it]) for it in items}
    e_last = {it: jnp.exp(g_last[it]) for it in items}

    state = [state_ref[hh] for hh in heads]
    for ch in range(n_ch):
        sb = [state[hh].astype(BF16) for hh in heads]
        w_s = [_dot(w[(hh, ch)], sb[hh]) for hh in heads]
        q_s = [_dot(qb[(hh, ch)], sb[hh]) for hh in heads]
        v_new = [u[(hh, ch)] - w_s[hh] for hh in heads]
        v_dec = [(v_new[hh] * e_dec[(hh, ch)]).astype(BF16) for hh in heads]
        upd = [_dot_tn(kb[(hh, ch)], v_dec[hh]) for hh in heads]
        intra = [_dot(attn[(hh, ch)], v_new[hh].astype(BF16)) for hh in heads]
        for hh in heads:
            oacc[rows(ch), hh * hd:(hh + 1) * hd] = e_col[(hh, ch)] * q_s[hh] + intra[hh]
        state = [state[hh] * e_last[(hh, ch)] + upd[hh] for hh in heads]
    for hh in heads:
        state_ref[hh] = state[hh]

    z = z_ref[...]
    ng = ng_ref[...]
    for hh in range(GDN_GROUP):
        ls = slice(hh * hd, (hh + 1) * hd)
        o_ref[:, ls] = ((_rms(oacc[:, ls]) * ng) * _silu(z[:, ls])).astype(o_ref.dtype)


def _gdn(p_gdn, p_ab, conv_w, alog2, dtb2, norm_g, *, batch, seq, tt=512):
    m = p_gdn.shape[0]
    gw = GDN_GROUP * GDN_HEAD_DIM
    n_groups = GDN_HEADS // GDN_GROUP
    nt = seq // tt

    def col(off):
        return pl.BlockSpec((tt, gw), lambda b, hg, t: (b * nt + t, off + hg))

    def cw(off):
        return pl.BlockSpec((GDN_CONV, gw), lambda b, hg, t: (0, off + hg))

    lane_spec = pl.BlockSpec((1, LANES), lambda b, hg, t: (0, hg))
    return pl.pallas_call(
        functools.partial(_gdn_kernel, tt=tt),
        out_shape=jax.ShapeDtypeStruct((m, GDN_HEADS * GDN_HEAD_DIM), BF16),
        grid=(batch, n_groups, nt),
        in_specs=[col(0), col(n_groups), col(2 * n_groups), col(3 * n_groups),
                  pl.BlockSpec((tt, LANES), lambda b, hg, t: (b * nt + t, hg)),
                  cw(0), cw(n_groups), cw(2 * n_groups),
                  lane_spec, lane_spec,
                  pl.BlockSpec((1, GDN_HEAD_DIM), lambda b, hg, t: (0, 0))],
        out_specs=pl.BlockSpec((tt, gw), lambda b, hg, t: (b * nt + t, hg)),
        scratch_shapes=[pltpu.VMEM((tt + GDN_HALO, gw), F32),
                        pltpu.VMEM((tt + GDN_HALO, gw), F32),
                        pltpu.VMEM((tt + GDN_HALO, gw), F32),
                        pltpu.VMEM((GDN_GROUP, GDN_HEAD_DIM, GDN_HEAD_DIM), F32),
                        pltpu.VMEM((tt, gw), F32)],
        compiler_params=_params("parallel", "parallel", "arbitrary"),
        name="gated_deltanet",
    )(p_gdn, p_gdn, p_gdn, p_gdn, p_ab, conv_w, conv_w, conv_w, alog2, dtb2,
      norm_g.reshape(1, GDN_HEAD_DIM))


def _moba_kernel(q_ref, k_ref, v_ref, o_ref, kmean_ref, *, nb):
    blk = MOBA_BLOCK
    exp2_scale = (MOBA_HEAD_DIM ** -0.5) * 1.4426950408889634

    for n in range(nb - 1):
        kblk = k_ref[n * blk:(n + 1) * blk, :].astype(F32)
        kmean_ref[n:n + 1, :] = jnp.sum(kblk, axis=0, keepdims=True) * (1.0 / blk)

    ri = lax.broadcasted_iota(jnp.int32, (blk, blk), 0)
    ci = lax.broadcasted_iota(jnp.int32, (blk, blk), 1)
    causal = ci <= ri
    owns = range(nb)

    q = [q_ref[own * blk:(own + 1) * blk, :] for own in owns]
    s = [_dot_nt(q[own], k_ref[0:(own + 1) * blk, :]) for own in owns]

    sel = {}
    for own in owns:
        if own <= MOBA_TOPK:
            continue
        kmean = kmean_ref[0:own, :]
        km_hi = kmean.astype(BF16)
        km_lo = (kmean - km_hi.astype(F32)).astype(BF16)
        gate = _dot_nt(q[own], km_hi) + _dot_nt(q[own], km_lo)
        lane = lax.broadcasted_iota(jnp.int32, (blk, own), 1)
        rank = jnp.zeros((blk, own), jnp.int32)
        for mth in range(own):
            gm = gate[:, mth:mth + 1]
            beats = (gm > gate) | ((gm == gate) & (lane > mth))
            rank = rank + jnp.where(beats, 1, 0)
        sel[own] = rank < MOBA_TOPK

    pieces = []
    for own in owns:
        past = [s[own][:, n * blk:(n + 1) * blk] for n in range(own)]
        if own in sel:
            past = [jnp.where(sel[own][:, n:n + 1], past[n], MASKED) for n in range(own)]
        pieces.append(past + [jnp.where(causal, s[own][:, own * blk:], MASKED)])
    m_run = [jnp.max(functools.reduce(jnp.maximum, pieces[own]), axis=-1, keepdims=True)
             for own in owns]
    probs = [[jnp.exp2((pc - m_run[own]) * exp2_scale) for pc in pieces[own]] for own in owns]
    denom = [jnp.sum(functools.reduce(lambda a, b: a + b, probs[own]), axis=-1, keepdims=True)
             for own in owns]
    acc = [_dot(jnp.concatenate([pr.astype(BF16) for pr in probs[own]], axis=1),
                v_ref[0:(own + 1) * blk, :]) for own in owns]
    for own in owns:
        o_ref[own * blk:(own + 1) * blk, :] = (acc[own] / denom[own]).astype(o_ref.dtype)


def _moba(p_qk, p_v, *, batch, seq):
    m = p_qk.shape[0]
    nb = seq // MOBA_BLOCK
    hd = MOBA_HEAD_DIM
    return pl.pallas_call(
        functools.partial(_moba_kernel, nb=nb),
        out_shape=jax.ShapeDtypeStruct((m, MOBA_HEADS * hd), BF16),
        grid=(batch, MOBA_HEADS),
        in_specs=[pl.BlockSpec((seq, hd), lambda b, h: (b, h)),
                  pl.BlockSpec((seq, hd), lambda b, h: (b, MOBA_HEADS + h)),
                  pl.BlockSpec((seq, hd), lambda b, h: (b, h))],
        out_specs=pl.BlockSpec((seq, hd), lambda b, h: (b, h)),
        scratch_shapes=[pltpu.VMEM((nb, hd), F32)],
        compiler_params=_params("parallel", "parallel"),
        name="moba_attention",
    )(p_qk, p_qk, p_v)


def _out_proj_kernel(x_ref, mod_ref, post_ref, next_ref, yg_ref, ym_ref, w_ref, o_ref, hn_ref, w16,
                     *, row):
    @pl.when(pl.program_id(0) == 0)
    def _():
        w16[...] = w_ref[...].astype(BF16)

    kg = yg_ref.shape[1]
    y = _dot(yg_ref[...], w16[0:kg, :]) + _dot(ym_ref[...], w16[kg:, :])
    ga = mod_ref[0, row + 2:row + 3, :]
    out = x_ref[...] + ga * (_rms(y) * post_ref[...])
    o_ref[...] = out
    hn_ref[...] = _mod_norm(out, mod_ref, next_ref, row + 3).astype(hn_ref.dtype)


def _out_proj(x2, mod, row, post_g, next_g, y_gdn, y_moba, w_out, *, seq, tm=512):
    m, d = x2.shape
    per_b = seq // tm
    kg, km = y_gdn.shape[1], y_moba.shape[1]
    tile = pl.BlockSpec((tm, d), lambda i: (i, 0))
    row_spec = pl.BlockSpec((1, d), lambda i: (0, 0))
    return pl.pallas_call(
        functools.partial(_out_proj_kernel, row=row),
        out_shape=(jax.ShapeDtypeStruct((m, d), F32), jax.ShapeDtypeStruct((m, d), BF16)),
        grid=(m // tm,),
        in_specs=[tile,
                  pl.BlockSpec((1, N_MOD, d), lambda i: (i // per_b, 0, 0)),
                  row_spec, row_spec,
                  pl.BlockSpec((tm, kg), lambda i: (i, 0)),
                  pl.BlockSpec((tm, km), lambda i: (i, 0)),
                  pl.BlockSpec((kg + km, d), lambda i: (0, 0), pipeline_mode=pl.Buffered(1))],
        out_specs=(tile, tile),
        scratch_shapes=[pltpu.VMEM((kg + km, d), BF16)],
        compiler_params=_params("arbitrary"),
        name="mixer_out_proj",
    )(x2, mod, post_g.reshape(1, d), next_g.reshape(1, d), y_gdn, y_moba, w_out)


def _group_lanes(vec):
    groups = vec.reshape(GDN_HEADS // GDN_GROUP, GDN_GROUP)
    return jnp.pad(groups, ((0, 0), (0, LANES - GDN_GROUP))).reshape(1, -1)


def _gate_rows(w_a, w_b):
    d = w_a.shape[1]
    parts = []
    for hg in range(GDN_HEADS // GDN_GROUP):
        hs = slice(hg * GDN_GROUP, (hg + 1) * GDN_GROUP)
        parts += [w_a[hs], w_b[hs], jnp.zeros((LANES - 2 * GDN_GROUP, d), w_a.dtype)]
    return jnp.concatenate(parts, axis=0)


def kernel(x, c, w_ada, b_ada, ffn1_pre_g, ffn1_post_g, ffn1_w_gate, ffn1_w_up, ffn1_w_down, mix_pre_g, mix_post_g, w_in, gdn_conv_w, gdn_a_log, gdn_dt_bias, gdn_norm_g, w_out, ffn2_pre_g, ffn2_post_g, ffn2_w_gate, ffn2_w_up, ffn2_w_down):
    batch, seq, d = x.shape
    depth = w_ada.shape[0]
    gdn_w = GDN_HEADS * GDN_HEAD_DIM
    x2 = x.reshape(batch * seq, d)
    for l in range(depth):
        mod = _adaln(c, w_ada[l], b_ada[l]).reshape(batch, N_MOD, d)

        h1 = _norm_mod(x2, mod, 0, ffn1_pre_g[l], seq=seq)
        x2, h2 = _ffn(x2, h1, mod, 0, ffn1_post_g[l], ffn1_w_gate[l], ffn1_w_up[l], ffn1_w_down[l],
                      seq=seq, next_g=mix_pre_g[l])

        w_t = jnp.swapaxes(w_in[l], 0, 1)
        o_a = 4 * gdn_w
        o_b = o_a + GDN_HEADS
        w_ab_t = _gate_rows(w_t[o_a:o_b], w_t[o_b:o_b + GDN_HEADS])
        p_gdn, p_qk, p_v, p_ab = _in_proj(h2, w_t, w_ab_t, seq=seq)
        y_gdn = _gdn(p_gdn, p_ab, gdn_conv_w[l], _group_lanes(gdn_a_log[l]),
                     _group_lanes(gdn_dt_bias[l]), gdn_norm_g[l], batch=batch, seq=seq)
        y_moba = _moba(p_qk, p_v, batch=batch, seq=seq)
        x2, h3 = _out_proj(x2, mod, 3, mix_post_g[l], ffn2_pre_g[l], y_gdn, y_moba, w_out[l], seq=seq)

        x2 = _ffn(x2, h3, mod, 6, ffn2_post_g[l], ffn2_w_gate[l], ffn2_w_up[l], ffn2_w_down[l],
                  seq=seq)
    return x2.reshape(batch, seq, d)
```

```python
import functools

import jax
import jax.numpy as jnp
from jax import lax
from jax.experimental import pallas as pl
from jax.experimental.pallas import tpu as pltpu

F32 = jnp.float32
BF16 = jnp.bfloat16

NORM_EPS = 1e-6
N_MOD = 9
GDN_HEADS = 8
GDN_HEAD_DIM = 128
GDN_CONV = 4
GDN_CHUNK = 128
MOBA_HEADS = 8
MOBA_HEAD_DIM = 128
MOBA_BLOCK = 256
MOBA_TOPK = 3
ROPE_DIM = MOBA_HEAD_DIM // 4
ROPE_THETA = 500000.0

LANES = 128
SUBLANES = 8
VMEM_LIMIT = 56 * 1024 * 1024
MASKED = -1e30


def _dot(a, b):
    return jnp.dot(a, b, preferred_element_type=F32)


def _dot_nt(a, b):
    return lax.dot_general(a, b, (((1,), (1,)), ((), ())), preferred_element_type=F32)


def _dot_tn(a, b):
    return lax.dot_general(a, b, (((0,), (0,)), ((), ())), preferred_element_type=F32)


def _rms(x):
    return x * lax.rsqrt(jnp.mean(x * x, axis=-1, keepdims=True) + NORM_EPS)


def _silu(x):
    return x * jax.nn.sigmoid(x)


def _params(*sem):
    return pltpu.CompilerParams(dimension_semantics=sem, vmem_limit_bytes=VMEM_LIMIT)


def _adaln_kernel(c_ref, w_ref, b_ref, o_ref):
    o_ref[...] = _dot(_silu(c_ref[...]), w_ref[...]) + b_ref[...]


def _adaln(c, w, b, *, tn=1024):
    bsz, d = c.shape
    n = w.shape[1]
    rows = -(-bsz // SUBLANES) * SUBLANES
    c_pad = jnp.pad(c, ((0, rows - bsz), (0, 0)))
    out = pl.pallas_call(
        _adaln_kernel,
        out_shape=jax.ShapeDtypeStruct((rows, n), F32),
        grid=(n // tn,),
        in_specs=[pl.BlockSpec((rows, d), lambda j: (0, 0)),
                  pl.BlockSpec((d, tn), lambda j: (0, j)),
                  pl.BlockSpec((1, tn), lambda j: (0, j))],
        out_specs=pl.BlockSpec((rows, tn), lambda j: (0, j)),
        compiler_params=_params("parallel"),
        name="adaln",
    )(c_pad, w, b.reshape(1, n))
    return out[:bsz]


def _mod_norm(x, mod_ref, g_ref, row):
    sh = mod_ref[0, row:row + 1, :]
    sc = mod_ref[0, row + 1:row + 2, :]
    return (_rms(x) * g_ref[...]) * (1.0 + sc) + sh


def _norm_mod_kernel(x_ref, mod_ref, g_ref, h_ref, *, row):
    h_ref[...] = _mod_norm(x_ref[...], mod_ref, g_ref, row).astype(h_ref.dtype)


def _norm_mod(x2, mod, row, g, *, seq, tm=1024):
    m, d = x2.shape
    per_b = seq // tm
    return pl.pallas_call(
        functools.partial(_norm_mod_kernel, row=row),
        out_shape=jax.ShapeDtypeStruct((m, d), BF16),
        grid=(m // tm,),
        in_specs=[pl.BlockSpec((tm, d), lambda i: (i, 0)),
                  pl.BlockSpec((1, N_MOD, d), lambda i: (i // per_b, 0, 0)),
                  pl.BlockSpec((1, d), lambda i: (0, 0))],
        out_specs=pl.BlockSpec((tm, d), lambda i: (i, 0)),
        compiler_params=_params("parallel"),
        name="norm_mod",
    )(x2, mod, g.reshape(1, d))


def _ffn_up_kernel(h_ref, wg_ref, wu_ref, wd_ref, a_ref, wd16_ref, wg16, wu16):
    @pl.when(pl.program_id(1) == 0)
    def _():
        wg16[...] = wg_ref[...].astype(BF16)
        wu16[...] = wu_ref[...].astype(BF16)
        wd16_ref[...] = wd_ref[...].astype(BF16)

    h = h_ref[...]
    a_ref[...] = (_silu(_dot(h, wg16[...])) * _dot(h, wu16[...])).astype(a_ref.dtype)


def _ffn_down_kernel(*refs, row, emit_next):
    if emit_next:
        x_ref, mod_ref, post_ref, next_ref, a_ref, wd_ref, o_ref, hn_ref = refs
    else:
        x_ref, mod_ref, post_ref, a_ref, wd_ref, o_ref = refs
    k = pl.program_id(1)

    @pl.when(k == 0)
    def _():
        o_ref[...] = jnp.zeros_like(o_ref)

    o_ref[...] += _dot(a_ref[...], wd_ref[...])

    @pl.when(k == pl.num_programs(1) - 1)
    def _():
        ga = mod_ref[0, row + 2:row + 3, :]
        out = x_ref[...] + 0.5 * ga * (_rms(o_ref[...]) * post_ref[...])
        o_ref[...] = out
        if emit_next:
            hn_ref[...] = _mod_norm(out, mod_ref, next_ref, row + 3).astype(hn_ref.dtype)


def _ffn(x2, h, mod, row, post_g, wg, wu, wd, *, seq, next_g=None, tm_up=1024, tf=512, tm_down=512,
         k_steps=2):
    m, d = x2.shape
    dff = wg.shape[1]
    act, wd16 = pl.pallas_call(
        _ffn_up_kernel,
        out_shape=(jax.ShapeDtypeStruct((m, dff), BF16), jax.ShapeDtypeStruct((dff, d), BF16)),
        grid=(dff // tf, m // tm_up),
        in_specs=[pl.BlockSpec((tm_up, d), lambda j, i: (i, 0)),
                  pl.BlockSpec((d, tf), lambda j, i: (0, j)),
                  pl.BlockSpec((d, tf), lambda j, i: (0, j)),
                  pl.BlockSpec((tf, d), lambda j, i: (j, 0))],
        out_specs=(pl.BlockSpec((tm_up, tf), lambda j, i: (i, j)),
                   pl.BlockSpec((tf, d), lambda j, i: (j, 0))),
        scratch_shapes=[pltpu.VMEM((d, tf), BF16), pltpu.VMEM((d, tf), BF16)],
        compiler_params=_params("parallel", "arbitrary"),
        name="swiglu_up",
    )(h, wg, wu, wd)

    tm = tm_down
    tk = dff // k_steps
    per_b = seq // tm
    emit_next = next_g is not None
    row_spec = pl.BlockSpec((1, d), lambda i, k: (0, 0))
    tile_spec = pl.BlockSpec((tm, d), lambda i, k: (i, 0))
    gains = [post_g.reshape(1, d)] + ([next_g.reshape(1, d)] if emit_next else [])
    out_shape = [jax.ShapeDtypeStruct((m, d), F32)] + ([jax.ShapeDtypeStruct((m, d), BF16)] if emit_next else [])
    outs = pl.pallas_call(
        functools.partial(_ffn_down_kernel, row=row, emit_next=emit_next),
        out_shape=out_shape,
        grid=(m // tm, k_steps),
        in_specs=[tile_spec, pl.BlockSpec((1, N_MOD, d), lambda i, k: (i // per_b, 0, 0))]
                 + [row_spec] * len(gains)
                 + [pl.BlockSpec((tm, tk), lambda i, k: (i, k)),
                    pl.BlockSpec((tk, d), lambda i, k: (k, 0))],
        out_specs=[tile_spec] * len(out_shape),
        compiler_params=_params("parallel", "arbitrary"),
        name="swiglu_down",
    )(x2, mod, *gains, act, wd16)
    return outs if emit_next else outs[0]


def _proj_gdn_kernel(h_ref, w_ref, o_ref, w16):
    @pl.when(pl.program_id(1) == 0)
    def _():
        w16[...] = w_ref[...].astype(BF16)

    o_ref[...] = _dot_nt(h_ref[...], w16[...])


def _proj_qk_kernel(h_ref, w_ref, cos_ref, sa_ref, sb_ref, o_ref, w16):
    @pl.when(pl.program_id(1) == 0)
    def _():
        w16[...] = w_ref[...].astype(BF16)

    y = _dot_nt(h_ref[...], w16[...])
    cos, sa, sb = cos_ref[...], sa_ref[...], sb_ref[...]
    half = ROPE_DIM // 2
    for hd in range(y.shape[1] // LANES):
        yh = y[:, hd * LANES:(hd + 1) * LANES]
        rot = (yh * cos + pltpu.roll(yh, LANES - half, 1) * sa + pltpu.roll(yh, half, 1) * sb)
        o_ref[:, hd * LANES:(hd + 1) * LANES] = rot.astype(o_ref.dtype)


def _proj_v_kernel(h_ref, wv_ref, wab_ref, v_ref, ab_ref, wv16):
    @pl.when(pl.program_id(0) == 0)
    def _():
        wv16[...] = wv_ref[...].astype(BF16)

    h = h_ref[...]
    v_ref[...] = _dot_nt(h, wv16[...]).astype(v_ref.dtype)
    ab_ref[...] = _dot_nt(h, wab_ref[...].astype(BF16))


def _rope_tables(seq):
    half = ROPE_DIM // 2
    inv_freq = ROPE_THETA ** (-jnp.arange(0, ROPE_DIM, 2, dtype=F32) / ROPE_DIM)
    ang = jnp.arange(seq).astype(F32)[:, None] * inv_freq[None, :]
    cos, sin = jnp.cos(ang), jnp.sin(ang)
    rest = LANES - 2 * half
    zeros = jnp.zeros((seq, half), F32)
    tail0 = jnp.zeros((seq, rest), F32)
    cos_t = jnp.concatenate([cos, cos, jnp.ones((seq, rest), F32)], axis=1)
    sa_t = jnp.concatenate([-sin, zeros, tail0], axis=1)
    sb_t = jnp.concatenate([zeros, sin, tail0], axis=1)
    return cos_t, sa_t, sb_t


def _in_proj(h2, w_t, w_ab_t, *, seq, tm=1024, tn=1024):
    m, d = h2.shape
    per_b = seq // tm
    gdn_w = GDN_HEADS * GDN_HEAD_DIM
    moba_w = MOBA_HEADS * MOBA_HEAD_DIM
    n_gdn = 4 * gdn_w
    moba_row0 = n_gdn + 2 * GDN_HEADS
    h_spec = pl.BlockSpec((tm, d), lambda j, i: (i, 0))
    o_spec = pl.BlockSpec((tm, tn), lambda j, i: (i, j))
    w16 = pltpu.VMEM((tn, d), BF16)

    p_gdn = pl.pallas_call(
        _proj_gdn_kernel,
        out_shape=jax.ShapeDtypeStruct((m, n_gdn), F32),
        grid=(n_gdn // tn, m // tm),
        in_specs=[h_spec, pl.BlockSpec((tn, d), lambda j, i: (j, 0))],
        out_specs=o_spec,
        scratch_shapes=[w16],
        compiler_params=_params("parallel", "arbitrary"),
        name="proj_gdn",
    )(h2, w_t)

    n_qk = 2 * moba_w
    cos_t, sa_t, sb_t = _rope_tables(seq)
    tab_spec = pl.BlockSpec((tm, LANES), lambda j, i: (i % per_b, 0))
    p_qk = pl.pallas_call(
        _proj_qk_kernel,
        out_shape=jax.ShapeDtypeStruct((m, n_qk), BF16),
        grid=(n_qk // tn, m // tm),
        in_specs=[h_spec,
                  pl.BlockSpec((pl.Element(tn), pl.Element(d)),
                               lambda j, i: ((moba_row0 // SUBLANES + j * (tn // SUBLANES)) * SUBLANES, 0)),
                  tab_spec, tab_spec, tab_spec],
        out_specs=o_spec,
        scratch_shapes=[w16],
        compiler_params=_params("parallel", "arbitrary"),
        name="proj_moba_qk",
    )(h2, w_t, cos_t, sa_t, sb_t)

    n_ab = w_ab_t.shape[0]
    p_v, p_ab = pl.pallas_call(
        _proj_v_kernel,
        out_shape=(jax.ShapeDtypeStruct((m, moba_w), BF16), jax.ShapeDtypeStruct((m, n_ab), F32)),
        grid=(m // tm,),
        in_specs=[pl.BlockSpec((tm, d), lambda i: (i, 0)),
                  pl.BlockSpec((pl.Element(moba_w), pl.Element(d)), lambda i: (moba_row0 + n_qk, 0),
                               pipeline_mode=pl.Buffered(1)),
                  pl.BlockSpec((n_ab, d), lambda i: (0, 0))],
        out_specs=(pl.BlockSpec((tm, moba_w), lambda i: (i, 0)),
                   pl.BlockSpec((tm, n_ab), lambda i: (i, 0))),
        scratch_shapes=[pltpu.VMEM((moba_w, d), BF16)],
        compiler_params=_params("arbitrary"),
        name="proj_moba_v_gates",
    )(h2, w_t, w_ab_t)
    return p_gdn, p_qk, p_v, p_ab


GDN_GROUP = 4
GDN_HALO = SUBLANES


def _gdn_kernel(q_ref, k_ref, v_ref, z_ref, ab_ref, cwq_ref, cwk_ref, cwv_ref, alog_ref, dtb_ref,
                ng_ref, o_ref, xq, xk, xv, state_ref, oacc, *, tt):
    t = pl.program_id(2)
    c = GDN_CHUNK
    hd = GDN_HEAD_DIM
    halo = GDN_HALO

    @pl.when(t == 0)
    def _():
        for buf in (xq, xk, xv):
            buf[0:halo, :] = jnp.zeros((halo, buf.shape[1]), F32)
        state_ref[...] = jnp.zeros_like(state_ref)

    @pl.when(t > 0)
    def _():
        for buf in (xq, xk, xv):
            buf[0:halo, :] = buf[tt:tt + halo, :]

    xq[halo:halo + tt, :] = q_ref[...]
    xk[halo:halo + tt, :] = k_ref[...]
    xv[halo:halo + tt, :] = v_ref[...]

    def conv_silu(buf, cw_ref):
        base = halo - (GDN_CONV - 1)
        acc = buf[pl.ds(base, tt), :] * cw_ref[0:1, :]
        for j in range(1, GDN_CONV):
            acc = acc + buf[pl.ds(base + j, tt), :] * cw_ref[j:j + 1, :]
        return _silu(acc)

    qc = conv_silu(xq, cwq_ref)
    kc = conv_silu(xk, cwk_ref)
    vc = conv_silu(xv, cwv_ref)

    ng2 = 2 * GDN_GROUP
    ab_t = ab_ref[...].T[0:ng2, :]
    pre = ab_t + dtb_ref[...]
    softplus = jnp.maximum(pre, 0.0) + jnp.log1p(jnp.exp(-jnp.abs(pre)))
    g_t = -jnp.exp(alog_ref[...]) * softplus
    lanec = lax.broadcasted_iota(jnp.int32, (ng2, tt), 1) & (c - 1)
    shift = 1
    while shift < c:
        g_t = g_t + jnp.where(lanec >= shift, pltpu.roll(g_t, shift, 1), 0.0)
        shift *= 2
    is_decay = lax.broadcasted_iota(jnp.int32, (ng2, tt), 0) < GDN_GROUP
    gcum_t = jnp.where(is_decay, g_t, jax.nn.sigmoid(ab_t))
    beta_t = gcum_t
    gcum = jnp.concatenate([gcum_t, jnp.zeros((LANES - ng2, tt), F32)], axis=0).T
    beta = gcum

    ri = lax.broadcasted_iota(jnp.int32, (c, c), 0)
    ci = lax.broadcasted_iota(jnp.int32, (c, c), 1)
    tril = ri >= ci
    strict = ri > ci
    eye = (ri == ci).astype(F32)

    n_ch = tt // c
    heads = range(GDN_GROUP)
    items = [(hh, ch) for hh in heads for ch in range(n_ch)]
    qn, kn, vn = [], [], []
    for hh in heads:
        ls = slice(hh * hd, (hh + 1) * hd)
        qh, kh = qc[:, ls], kc[:, ls]
        qn.append(((qh * lax.rsqrt(jnp.sum(qh * qh, axis=-1, keepdims=True) + NORM_EPS))
                   * (hd ** -0.5)).astype(BF16))
        kn.append((kh * lax.rsqrt(jnp.sum(kh * kh, axis=-1, keepdims=True) + NORM_EPS)).astype(BF16))
        vn.append(vc[:, ls].astype(BF16))

    def rows(ch):
        return slice(ch * c, (ch + 1) * c)

    kb = {(hh, ch): kn[hh][rows(ch)] for hh, ch in items}
    qb = {(hh, ch): qn[hh][rows(ch)] for hh, ch in items}
    vb = {(hh, ch): vn[hh][rows(ch)] for hh, ch in items}
    g_col = {(hh, ch): gcum[rows(ch), hh:hh + 1] for hh, ch in items}
    g_row = {(hh, ch): gcum_t[hh:hh + 1, rows(ch)] for hh, ch in items}
    b_col = {(hh, ch): beta[rows(ch), GDN_GROUP + hh:GDN_GROUP + hh + 1] for hh, ch in items}
    b_row = {(hh, ch): beta_t[GDN_GROUP + hh:GDN_GROUP + hh + 1, rows(ch)] for hh, ch in items}
    g_last = {(hh, ch): gcum[ch * c + c - 1:ch * c + c, hh:hh + 1] for hh, ch in items}

    decay = {it: jnp.exp(jnp.where(tril, g_col[it] - g_row[it], MASKED)) for it in items}
    kk = {it: _dot_nt(kb[it], kb[it]) for it in items}
    qk = {it: _dot_nt(qb[it], kb[it]) for it in items}
    low = {it: jnp.where(strict, b_col[it] * kk[it] * decay[it], 0.0) for it in items}
    attn = {it: jnp.where(tril, qk[it] * decay[it], 0.0).astype(BF16) for it in items}

    inv = {it: eye - jnp.where((ri >> 1) == (ci >> 1), low[it], 0.0) for it in items}
    size = 4
    while size <= c:
        bits = size.bit_length() - 1
        off = ((ri >> bits) == (ci >> bits)) & ((ri >> (bits - 1)) != (ci >> (bits - 1)))
        ib = {it: inv[it].astype(BF16) for it in items}
        right = {it: _dot(jnp.where(off, low[it], 0.0).astype(BF16), ib[it]).astype(BF16)
                 for it in items}
        inv = {it: inv[it] - _dot(ib[it], right[it]) for it in items}
        size *= 2

    tb = {it: inv[it] * b_row[it] for it in items}
    u = {it: _dot(tb[it].astype(BF16), vb[it]) for it in items}
    w = {it: _dot((tb[it] * jnp.exp(g_row[it])).astype(BF16), kb[it]).astype(BF16) for it in items}
    e_col = {it: jnp.exp(g_col[it]) for it in items}
    e_dec = {it: jnp.exp(g_last[it] - g_col[it]) for it in items}
    e_last = {it: jnp.exp(g_last[it]) for it in items}

    state = [state_ref[hh] for hh in heads]
    for ch in range(n_ch):
        sb = [state[hh].astype(BF16) for hh in heads]
        w_s = [_dot(w[(hh, ch)], sb[hh]) for hh in heads]
        q_s = [_dot(qb[(hh, ch)], sb[hh]) for hh in heads]
        v_new = [u[(hh, ch)] - w_s[hh] for hh in heads]
        v_dec = [(v_new[hh] * e_dec[(hh, ch)]).astype(BF16) for hh in heads]
        upd = [_dot_tn(kb[(hh, ch)], v_dec[hh]) for hh in heads]
        intra = [_dot(attn[(hh, ch)], v_new[hh].astype(BF16)) for hh in heads]
        for hh in heads:
            oacc[rows(ch), hh * hd:(hh + 1) * hd] = e_col[(hh, ch)] * q_s[hh] + intra[hh]
        state = [state[hh] * e_last[(hh, ch)] + upd[hh] for hh in heads]
    for hh in heads:
        state_ref[hh] = state[hh]

    z = z_ref[...]
    ng = ng_ref[...]
    for hh in range(GDN_GROUP):
        ls = slice(hh * hd, (hh + 1) * hd)
        o_ref[:, ls] = ((_rms(oacc[:, ls]) * ng) * _silu(z[:, ls])).astype(o_ref.dtype)


def _gdn(p_gdn, p_ab, conv_w, alog2, dtb2, norm_g, *, batch, seq, tt=512):
    m = p_gdn.shape[0]
    gw = GDN_GROUP * GDN_HEAD_DIM
    n_groups = GDN_HEADS // GDN_GROUP
    nt = seq // tt

    def col(off):
        return pl.BlockSpec((tt, gw), lambda b, hg, t: (b * nt + t, off + hg))

    def cw(off):
        return pl.BlockSpec((GDN_CONV, gw), lambda b, hg, t: (0, off + hg))

    lane_spec = pl.BlockSpec((2 * GDN_GROUP, 1), lambda b, hg, t: (hg, 0))
    return pl.pallas_call(
        functools.partial(_gdn_kernel, tt=tt),
        out_shape=jax.ShapeDtypeStruct((m, GDN_HEADS * GDN_HEAD_DIM), BF16),
        grid=(batch, n_groups, nt),
        in_specs=[col(0), col(n_groups), col(2 * n_groups), col(3 * n_groups),
                  pl.BlockSpec((tt, LANES), lambda b, hg, t: (b * nt + t, hg)),
                  cw(0), cw(n_groups), cw(2 * n_groups),
                  lane_spec, lane_spec,
                  pl.BlockSpec((1, GDN_HEAD_DIM), lambda b, hg, t: (0, 0))],
        out_specs=pl.BlockSpec((tt, gw), lambda b, hg, t: (b * nt + t, hg)),
        scratch_shapes=[pltpu.VMEM((tt + GDN_HALO, gw), F32),
                        pltpu.VMEM((tt + GDN_HALO, gw), F32),
                        pltpu.VMEM((tt + GDN_HALO, gw), F32),
                        pltpu.VMEM((GDN_GROUP, GDN_HEAD_DIM, GDN_HEAD_DIM), F32),
                        pltpu.VMEM((tt, gw), F32)],
        compiler_params=_params("parallel", "parallel", "arbitrary"),
        name="gated_deltanet",
    )(p_gdn, p_gdn, p_gdn, p_gdn, p_ab, conv_w, conv_w, conv_w, alog2, dtb2,
      norm_g.reshape(1, GDN_HEAD_DIM))


def _moba_kernel(q_ref, k_ref, v_ref, o_ref, kmean_ref, *, nb):
    blk = MOBA_BLOCK
    exp2_scale = (MOBA_HEAD_DIM ** -0.5) * 1.4426950408889634

    for n in range(nb):
        kblk = k_ref[n * blk:(n + 1) * blk, :].astype(F32)
        kmean_ref[n:n + 1, :] = jnp.sum(kblk, axis=0, keepdims=True) * (1.0 / blk)

    ri = lax.broadcasted_iota(jnp.int32, (blk, blk), 0)
    ci = lax.broadcasted_iota(jnp.int32, (blk, blk), 1)
    causal = ci <= ri

    def query(own):
        return q_ref[own * blk:(own + 1) * blk, :]

    def scores(own):
        return _dot_nt(query(own), k_ref[0:(own + 1) * blk, :])

    def selection(own):
        if own <= MOBA_TOPK:
            return None
        kmean = kmean_ref[...]
        km_hi = kmean.astype(BF16)
        km_lo = (kmean - km_hi.astype(F32)).astype(BF16)
        gate = _dot_nt(km_hi, query(own)) + _dot_nt(km_lo, query(own))
        row = lax.broadcasted_iota(jnp.int32, (nb, blk), 0)
        rank = jnp.zeros((nb, blk), jnp.int32)
        for mth in range(own):
            gm = gate[mth:mth + 1, :]
            beats = (gm > gate) | ((gm == gate) & (row > mth))
            rank = rank + jnp.where(beats, 1, 0)
        chosen = jnp.where(rank < MOBA_TOPK, 1.0, 0.0)
        chosen = jnp.concatenate([chosen, jnp.zeros((LANES - nb, blk), F32)], axis=0)
        return chosen.T > 0.5

    s_next, sel_next = scores(0), selection(0)
    for own in range(nb):
        s, sel = s_next, sel_next
        if own + 1 < nb:
            s_next, sel_next = scores(own + 1), selection(own + 1)
        past = [s[:, n * blk:(n + 1) * blk] for n in range(own)]
        if sel is not None:
            past = [jnp.where(sel[:, n:n + 1], past[n], MASKED) for n in range(own)]
        pieces = past + [jnp.where(causal, s[:, own * blk:], MASKED)]
        m_run = jnp.max(functools.reduce(jnp.maximum, pieces), axis=-1, keepdims=True)
        probs = [jnp.exp2((pc - m_run) * exp2_scale) for pc in pieces]
        denom = jnp.sum(functools.reduce(lambda a, b: a + b, probs), axis=-1, keepdims=True)
        acc = _dot(jnp.concatenate([pr.astype(BF16) for pr in probs], axis=1),
                   v_ref[0:(own + 1) * blk, :])
        o_ref[own * blk:(own + 1) * blk, :] = (acc / denom).astype(o_ref.dtype)


def _moba(p_qk, p_v, *, batch, seq):
    m = p_qk.shape[0]
    nb = seq // MOBA_BLOCK
    hd = MOBA_HEAD_DIM
    return pl.pallas_call(
        functools.partial(_moba_kernel, nb=nb),
        out_shape=jax.ShapeDtypeStruct((m, MOBA_HEADS * hd), BF16),
        grid=(batch, MOBA_HEADS),
        in_specs=[pl.BlockSpec((seq, hd), lambda b, h: (b, h)),
                  pl.BlockSpec((seq, hd), lambda b, h: (b, MOBA_HEADS + h)),
                  pl.BlockSpec((seq, hd), lambda b, h: (b, h))],
        out_specs=pl.BlockSpec((seq, hd), lambda b, h: (b, h)),
        scratch_shapes=[pltpu.VMEM((nb, hd), F32)],
        compiler_params=_params("parallel", "parallel"),
        name="moba_attention",
    )(p_qk, p_qk, p_v)


def _out_proj_kernel(x_ref, mod_ref, post_ref, next_ref, yg_ref, ym_ref, w_ref, o_ref, hn_ref, w16,
                     *, row):
    @pl.when(pl.program_id(0) == 0)
    def _():
        w16[...] = w_ref[...].astype(BF16)

    kg = yg_ref.shape[1]
    y = _dot(yg_ref[...], w16[0:kg, :]) + _dot(ym_ref[...], w16[kg:, :])
    ga = mod_ref[0, row + 2:row + 3, :]
    out = x_ref[...] + ga * (_rms(y) * post_ref[...])
    o_ref[...] = out
    hn_ref[...] = _mod_norm(out, mod_ref, next_ref, row + 3).astype(hn_ref.dtype)


def _out_proj(x2, mod, row, post_g, next_g, y_gdn, y_moba, w_out, *, seq, tm=512):
    m, d = x2.shape
    per_b = seq // tm
    kg, km = y_gdn.shape[1], y_moba.shape[1]
    tile = pl.BlockSpec((tm, d), lambda i: (i, 0))
    row_spec = pl.BlockSpec((1, d), lambda i: (0, 0))
    return pl.pallas_call(
        functools.partial(_out_proj_kernel, row=row),
        out_shape=(jax.ShapeDtypeStruct((m, d), F32), jax.ShapeDtypeStruct((m, d), BF16)),
        grid=(m // tm,),
        in_specs=[tile,
                  pl.BlockSpec((1, N_MOD, d), lambda i: (i // per_b, 0, 0)),
                  row_spec, row_spec,
                  pl.BlockSpec((tm, kg), lambda i: (i, 0)),
                  pl.BlockSpec((tm, km), lambda i: (i, 0)),
                  pl.BlockSpec((kg + km, d), lambda i: (0, 0), pipeline_mode=pl.Buffered(1))],
        out_specs=(tile, tile),
        scratch_shapes=[pltpu.VMEM((kg + km, d), BF16)],
        compiler_params=_params("arbitrary"),
        name="mixer_out_proj",
    )(x2, mod, post_g.reshape(1, d), next_g.reshape(1, d), y_gdn, y_moba, w_out)


def _group_rows(vec):
    groups = vec.reshape(GDN_HEADS // GDN_GROUP, GDN_GROUP)
    return jnp.pad(groups, ((0, 0), (0, GDN_GROUP))).reshape(-1, 1)


def _gate_rows(w_a, w_b):
    d = w_a.shape[1]
    parts = []
    for hg in range(GDN_HEADS // GDN_GROUP):
        hs = slice(hg * GDN_GROUP, (hg + 1) * GDN_GROUP)
        parts += [w_a[hs], w_b[hs], jnp.zeros((LANES - 2 * GDN_GROUP, d), w_a.dtype)]
    return jnp.concatenate(parts, axis=0)


def kernel(x, c, w_ada, b_ada, ffn1_pre_g, ffn1_post_g, ffn1_w_gate, ffn1_w_up, ffn1_w_down, mix_pre_g, mix_post_g, w_in, gdn_conv_w, gdn_a_log, gdn_dt_bias, gdn_norm_g, w_out, ffn2_pre_g, ffn2_post_g, ffn2_w_gate, ffn2_w_up, ffn2_w_down):
    batch, seq, d = x.shape
    depth = w_ada.shape[0]
    gdn_w = GDN_HEADS * GDN_HEAD_DIM
    x2 = x.reshape(batch * seq, d)
    for l in range(depth):
        mod = _adaln(c, w_ada[l], b_ada[l]).reshape(batch, N_MOD, d)

        h1 = _norm_mod(x2, mod, 0, ffn1_pre_g[l], seq=seq)
        x2, h2 = _ffn(x2, h1, mod, 0, ffn1_post_g[l], ffn1_w_gate[l], ffn1_w_up[l], ffn1_w_down[l],
                      seq=seq, next_g=mix_pre_g[l])

        w_t = jnp.swapaxes(w_in[l], 0, 1)
        o_a = 4 * gdn_w
        o_b = o_a + GDN_HEADS
        w_ab_t = _gate_rows(w_t[o_a:o_b], w_t[o_b:o_b + GDN_HEADS])
        p_gdn, p_qk, p_v, p_ab = _in_proj(h2, w_t, w_ab_t, seq=seq)
        y_gdn = _gdn(p_gdn, p_ab, gdn_conv_w[l], _group_rows(gdn_a_log[l]),
                     _group_rows(gdn_dt_bias[l]), gdn_norm_g[l], batch=batch, seq=seq)
        y_moba = _moba(p_qk, p_v, batch=batch, seq=seq)
        x2, h3 = _out_proj(x2, mod, 3, mix_post_g[l], ffn2_pre_g[l], y_gdn, y_moba, w_out[l], seq=seq)

        x2 = _ffn(x2, h3, mod, 6, ffn2_post_g[l], ffn2_w_gate[l], ffn2_w_up[l], ffn2_w_down[l],
                  seq=seq)
    return x2.reshape(batch, seq, d)
```

```python
import functools

import jax
import jax.numpy as jnp
from jax import lax
from jax.experimental import pallas as pl
from jax.experimental.pallas import tpu as pltpu

F32 = jnp.float32
BF16 = jnp.bfloat16

NORM_EPS = 1e-6
N_MOD = 9
GDN_HEADS = 8
GDN_HEAD_DIM = 128
GDN_CONV = 4
GDN_CHUNK = 128
MOBA_HEADS = 8
MOBA_HEAD_DIM = 128
MOBA_BLOCK = 256
MOBA_TOPK = 3
ROPE_DIM = MOBA_HEAD_DIM // 4
ROPE_THETA = 500000.0

LANES = 128
SUBLANES = 8
VMEM_LIMIT = 56 * 1024 * 1024
MASKED = -1e30


def _dot(a, b):
    return jnp.dot(a, b, preferred_element_type=F32)


def _dot_nt(a, b):
    return lax.dot_general(a, b, (((1,), (1,)), ((), ())), preferred_element_type=F32)


def _dot_tn(a, b):
    return lax.dot_general(a, b, (((0,), (0,)), ((), ())), preferred_element_type=F32)


def _rms(x):
    return x * lax.rsqrt(jnp.mean(x * x, axis=-1, keepdims=True) + NORM_EPS)


def _silu(x):
    return x * jax.nn.sigmoid(x)


def _params(*sem):
    return pltpu.CompilerParams(dimension_semantics=sem, vmem_limit_bytes=VMEM_LIMIT)


def _adaln_kernel(c_ref, w_ref, b_ref, o_ref):
    o_ref[...] = _dot(_silu(c_ref[...]), w_ref[...]) + b_ref[...]


def _adaln(c, w, b, *, tn=1024):
    bsz, d = c.shape
    n = w.shape[1]
    rows = -(-bsz // SUBLANES) * SUBLANES
    c_pad = jnp.pad(c, ((0, rows - bsz), (0, 0)))
    out = pl.pallas_call(
        _adaln_kernel,
        out_shape=jax.ShapeDtypeStruct((rows, n), F32),
        grid=(n // tn,),
        in_specs=[pl.BlockSpec((rows, d), lambda j: (0, 0)),
                  pl.BlockSpec((d, tn), lambda j: (0, j)),
                  pl.BlockSpec((1, tn), lambda j: (0, j))],
        out_specs=pl.BlockSpec((rows, tn), lambda j: (0, j)),
        compiler_params=_params("parallel"),
        name="adaln",
    )(c_pad, w, b.reshape(1, n))
    return out[:bsz]


def _mod_norm(x, mod_ref, g_ref, row):
    sh = mod_ref[0, row:row + 1, :]
    sc = mod_ref[0, row + 1:row + 2, :]
    return (_rms(x) * g_ref[...]) * (1.0 + sc) + sh


def _norm_mod_kernel(x_ref, mod_ref, g_ref, h_ref, *, row):
    h_ref[...] = _mod_norm(x_ref[...], mod_ref, g_ref, row).astype(h_ref.dtype)


def _norm_mod(x2, mod, row, g, *, seq, tm=1024):
    m, d = x2.shape
    per_b = seq // tm
    return pl.pallas_call(
        functools.partial(_norm_mod_kernel, row=row),
        out_shape=jax.ShapeDtypeStruct((m, d), BF16),
        grid=(m // tm,),
        in_specs=[pl.BlockSpec((tm, d), lambda i: (i, 0)),
                  pl.BlockSpec((1, N_MOD, d), lambda i: (i // per_b, 0, 0)),
                  pl.BlockSpec((1, d), lambda i: (0, 0))],
        out_specs=pl.BlockSpec((tm, d), lambda i: (i, 0)),
        compiler_params=_params("parallel"),
        name="norm_mod",
    )(x2, mod, g.reshape(1, d))


def _ffn_up_kernel(h_ref, wg_ref, wu_ref, wd_ref, a_ref, wd16_ref, wg16, wu16):
    @pl.when(pl.program_id(1) == 0)
    def _():
        wg16[...] = wg_ref[...].astype(BF16)
        wu16[...] = wu_ref[...].astype(BF16)
        wd16_ref[...] = wd_ref[...].astype(BF16)

    h = h_ref[...]
    a_ref[...] = (_silu(_dot(h, wg16[...])) * _dot(h, wu16[...])).astype(a_ref.dtype)


def _ffn_down_kernel(*refs, row, emit_next):
    if emit_next:
        x_ref, mod_ref, post_ref, next_ref, a_ref, wd_ref, o_ref, hn_ref = refs
    else:
        x_ref, mod_ref, post_ref, a_ref, wd_ref, o_ref = refs
    k = pl.program_id(1)

    @pl.when(k == 0)
    def _():
        o_ref[...] = jnp.zeros_like(o_ref)

    o_ref[...] += _dot(a_ref[...], wd_ref[...])

    @pl.when(k == pl.num_programs(1) - 1)
    def _():
        ga = mod_ref[0, row + 2:row + 3, :]
        out = x_ref[...] + 0.5 * ga * (_rms(o_ref[...]) * post_ref[...])
        o_ref[...] = out
        if emit_next:
            hn_ref[...] = _mod_norm(out, mod_ref, next_ref, row + 3).astype(hn_ref.dtype)


def _ffn(x2, h, mod, row, post_g, wg, wu, wd, *, seq, next_g=None, tm_up=1024, tf=512, tm_down=512,
         k_steps=2):
    m, d = x2.shape
    dff = wg.shape[1]
    act, wd16 = pl.pallas_call(
        _ffn_up_kernel,
        out_shape=(jax.ShapeDtypeStruct((m, dff), BF16), jax.ShapeDtypeStruct((dff, d), BF16)),
        grid=(dff // tf, m // tm_up),
        in_specs=[pl.BlockSpec((tm_up, d), lambda j, i: (i, 0)),
                  pl.BlockSpec((d, tf), lambda j, i: (0, j)),
                  pl.BlockSpec((d, tf), lambda j, i: (0, j)),
                  pl.BlockSpec((tf, d), lambda j, i: (j, 0))],
        out_specs=(pl.BlockSpec((tm_up, tf), lambda j, i: (i, j)),
                   pl.BlockSpec((tf, d), lambda j, i: (j, 0))),
        scratch_shapes=[pltpu.VMEM((d, tf), BF16), pltpu.VMEM((d, tf), BF16)],
        compiler_params=_params("parallel", "arbitrary"),
        name="swiglu_up",
    )(h, wg, wu, wd)

    tm = tm_down
    tk = dff // k_steps
    per_b = seq // tm
    emit_next = next_g is not None
    row_spec = pl.BlockSpec((1, d), lambda i, k: (0, 0))
    tile_spec = pl.BlockSpec((tm, d), lambda i, k: (i, 0))
    gains = [post_g.reshape(1, d)] + ([next_g.reshape(1, d)] if emit_next else [])
    out_shape = [jax.ShapeDtypeStruct((m, d), F32)] + ([jax.ShapeDtypeStruct((m, d), BF16)] if emit_next else [])
    outs = pl.pallas_call(
        functools.partial(_ffn_down_kernel, row=row, emit_next=emit_next),
        out_shape=out_shape,
        grid=(m // tm, k_steps),
        in_specs=[tile_spec, pl.BlockSpec((1, N_MOD, d), lambda i, k: (i // per_b, 0, 0))]
                 + [row_spec] * len(gains)
                 + [pl.BlockSpec((tm, tk), lambda i, k: (i, k)),
                    pl.BlockSpec((tk, d), lambda i, k: (k, 0))],
        out_specs=[tile_spec] * len(out_shape),
        compiler_params=_params("parallel", "arbitrary"),
        name="swiglu_down",
    )(x2, mod, *gains, act, wd16)
    return outs if emit_next else outs[0]


def _proj_gdn_kernel(h_ref, w_ref, o_ref, w16):
    @pl.when(pl.program_id(1) == 0)
    def _():
        w16[...] = w_ref[...].astype(BF16)

    o_ref[...] = _dot_nt(h_ref[...], w16[...])


def _proj_qk_kernel(h_ref, w_ref, cos_ref, sa_ref, sb_ref, o_ref, w16):
    @pl.when(pl.program_id(1) == 0)
    def _():
        w16[...] = w_ref[...].astype(BF16)

    y = _dot_nt(h_ref[...], w16[...])
    cos, sa, sb = cos_ref[...], sa_ref[...], sb_ref[...]
    half = ROPE_DIM // 2
    for hd in range(y.shape[1] // LANES):
        yh = y[:, hd * LANES:(hd + 1) * LANES]
        rot = (yh * cos + pltpu.roll(yh, LANES - half, 1) * sa + pltpu.roll(yh, half, 1) * sb)
        o_ref[:, hd * LANES:(hd + 1) * LANES] = rot.astype(o_ref.dtype)


def _proj_v_kernel(h_ref, wv_ref, wab_ref, v_ref, ab_ref, wv16):
    @pl.when(pl.program_id(0) == 0)
    def _():
        wv16[...] = wv_ref[...].astype(BF16)

    h = h_ref[...]
    v_ref[...] = _dot_nt(h, wv16[...]).astype(v_ref.dtype)
    ab_ref[...] = _dot_nt(h, wab_ref[...].astype(BF16))


def _rope_tables(seq):
    half = ROPE_DIM // 2
    inv_freq = ROPE_THETA ** (-jnp.arange(0, ROPE_DIM, 2, dtype=F32) / ROPE_DIM)
    ang = jnp.arange(seq).astype(F32)[:, None] * inv_freq[None, :]
    cos, sin = jnp.cos(ang), jnp.sin(ang)
    rest = LANES - 2 * half
    zeros = jnp.zeros((seq, half), F32)
    tail0 = jnp.zeros((seq, rest), F32)
    cos_t = jnp.concatenate([cos, cos, jnp.ones((seq, rest), F32)], axis=1)
    sa_t = jnp.concatenate([-sin, zeros, tail0], axis=1)
    sb_t = jnp.concatenate([zeros, sin, tail0], axis=1)
    return cos_t, sa_t, sb_t


def _in_proj(h2, w_t, w_ab_t, *, seq, tm=1024, tn=1024):
    m, d = h2.shape
    per_b = seq // tm
    gdn_w = GDN_HEADS * GDN_HEAD_DIM
    moba_w = MOBA_HEADS * MOBA_HEAD_DIM
    n_gdn = 4 * gdn_w
    moba_row0 = n_gdn + 2 * GDN_HEADS
    h_spec = pl.BlockSpec((tm, d), lambda j, i: (i, 0))
    o_spec = pl.BlockSpec((tm, tn), lambda j, i: (i, j))
    w16 = pltpu.VMEM((tn, d), BF16)

    p_gdn = pl.pallas_call(
        _proj_gdn_kernel,
        out_shape=jax.ShapeDtypeStruct((m, n_gdn), F32),
        grid=(n_gdn // tn, m // tm),
        in_specs=[h_spec, pl.BlockSpec((tn, d), lambda j, i: (j, 0))],
        out_specs=o_spec,
        scratch_shapes=[w16],
        compiler_params=_params("parallel", "arbitrary"),
        name="proj_gdn",
    )(h2, w_t)

    n_qk = 2 * moba_w
    cos_t, sa_t, sb_t = _rope_tables(seq)
    tab_spec = pl.BlockSpec((tm, LANES), lambda j, i: (i % per_b, 0))
    p_qk = pl.pallas_call(
        _proj_qk_kernel,
        out_shape=jax.ShapeDtypeStruct((m, n_qk), BF16),
        grid=(n_qk // tn, m // tm),
        in_specs=[h_spec,
                  pl.BlockSpec((pl.Element(tn), pl.Element(d)),
                               lambda j, i: ((moba_row0 // SUBLANES + j * (tn // SUBLANES)) * SUBLANES, 0)),
                  tab_spec, tab_spec, tab_spec],
        out_specs=o_spec,
        scratch_shapes=[w16],
        compiler_params=_params("parallel", "arbitrary"),
        name="proj_moba_qk",
    )(h2, w_t, cos_t, sa_t, sb_t)

    n_ab = w_ab_t.shape[0]
    p_v, p_ab = pl.pallas_call(
        _proj_v_kernel,
        out_shape=(jax.ShapeDtypeStruct((m, moba_w), BF16), jax.ShapeDtypeStruct((m, n_ab), F32)),
        grid=(m // tm,),
        in_specs=[pl.BlockSpec((tm, d), lambda i: (i, 0)),
                  pl.BlockSpec((pl.Element(moba_w), pl.Element(d)), lambda i: (moba_row0 + n_qk, 0),
                               pipeline_mode=pl.Buffered(1)),
                  pl.BlockSpec((n_ab, d), lambda i: (0, 0))],
        out_specs=(pl.BlockSpec((tm, moba_w), lambda i: (i, 0)),
                   pl.BlockSpec((tm, n_ab), lambda i: (i, 0))),
        scratch_shapes=[pltpu.VMEM((moba_w, d), BF16)],
        compiler_params=_params("arbitrary"),
        name="proj_moba_v_gates",
    )(h2, w_t, w_ab_t)
    return p_gdn, p_qk, p_v, p_ab


GDN_GROUP = 4
GDN_HALO = SUBLANES


def _gdn_kernel(q_ref, k_ref, v_ref, z_ref, ab_ref, cwq_ref, cwk_ref, cwv_ref, alog_ref, dtb_ref,
                ng_ref, o_ref, xq, xk, xv, state_ref, oacc, *, tt):
    t = pl.program_id(2)
    c = GDN_CHUNK
    hd = GDN_HEAD_DIM
    halo = GDN_HALO

    @pl.when(t == 0)
    def _():
        for buf in (xq, xk, xv):
            buf[0:halo, :] = jnp.zeros((halo, buf.shape[1]), F32)
        state_ref[...] = jnp.zeros_like(state_ref)

    @pl.when(t > 0)
    def _():
        for buf in (xq, xk, xv):
            buf[0:halo, :] = buf[tt:tt + halo, :]

    xq[halo:halo + tt, :] = q_ref[...]
    xk[halo:halo + tt, :] = k_ref[...]
    xv[halo:halo + tt, :] = v_ref[...]

    def conv_silu(buf, cw_ref):
        base = halo - (GDN_CONV - 1)
        acc = buf[pl.ds(base, tt), :] * cw_ref[0:1, :]
        for j in range(1, GDN_CONV):
            acc = acc + buf[pl.ds(base + j, tt), :] * cw_ref[j:j + 1, :]
        return _silu(acc)

    qc = conv_silu(xq, cwq_ref)
    kc = conv_silu(xk, cwk_ref)
    vc = conv_silu(xv, cwv_ref)

    ng2 = 2 * GDN_GROUP
    ab_t = ab_ref[...].T[0:ng2, :]
    pre = ab_t + dtb_ref[...]
    softplus = jnp.maximum(pre, 0.0) + jnp.log1p(jnp.exp(-jnp.abs(pre)))
    g_t = -jnp.exp(alog_ref[...]) * softplus
    lanec = lax.broadcasted_iota(jnp.int32, (ng2, tt), 1) & (c - 1)
    shift = 1
    while shift < c:
        g_t = g_t + jnp.where(lanec >= shift, pltpu.roll(g_t, shift, 1), 0.0)
        shift *= 2
    is_decay = lax.broadcasted_iota(jnp.int32, (ng2, tt), 0) < GDN_GROUP
    gcum_t = jnp.where(is_decay, g_t, jax.nn.sigmoid(ab_t))
    beta_t = gcum_t
    gcum = jnp.concatenate([gcum_t, jnp.zeros((LANES - ng2, tt), F32)], axis=0).T
    beta = gcum

    ri = lax.broadcasted_iota(jnp.int32, (c, c), 0)
    ci = lax.broadcasted_iota(jnp.int32, (c, c), 1)
    tril = ri >= ci
    strict = ri > ci
    eye = (ri == ci).astype(F32)

    n_ch = tt // c
    heads = range(GDN_GROUP)
    items = [(hh, ch) for hh in heads for ch in range(n_ch)]
    qn, kn, vn = [], [], []
    for hh in heads:
        ls = slice(hh * hd, (hh + 1) * hd)
        qh, kh = qc[:, ls], kc[:, ls]
        qn.append(((qh * lax.rsqrt(jnp.sum(qh * qh, axis=-1, keepdims=True) + NORM_EPS))
                   * (hd ** -0.5)).astype(BF16))
        kn.append((kh * lax.rsqrt(jnp.sum(kh * kh, axis=-1, keepdims=True) + NORM_EPS)).astype(BF16))
        vn.append(vc[:, ls].astype(BF16))

    def rows(ch):
        return slice(ch * c, (ch + 1) * c)

    kb = {(hh, ch): kn[hh][rows(ch)] for hh, ch in items}
    qb = {(hh, ch): qn[hh][rows(ch)] for hh, ch in items}
    vb = {(hh, ch): vn[hh][rows(ch)] for hh, ch in items}
    g_col = {(hh, ch): gcum[rows(ch), hh:hh + 1] for hh, ch in items}
    g_row = {(hh, ch): gcum_t[hh:hh + 1, rows(ch)] for hh, ch in items}
    b_col = {(hh, ch): beta[rows(ch), GDN_GROUP + hh:GDN_GROUP + hh + 1] for hh, ch in items}
    b_row = {(hh, ch): beta_t[GDN_GROUP + hh:GDN_GROUP + hh + 1, rows(ch)] for hh, ch in items}
    g_last = {(hh, ch): gcum[ch * c + c - 1:ch * c + c, hh:hh + 1] for hh, ch in items}

    decay = {it: jnp.exp(jnp.where(tril, g_col[it] - g_row[it], MASKED)) for it in items}
    kk = {it: _dot_nt(kb[it], kb[it]) for it in items}
    qk = {it: _dot_nt(qb[it], kb[it]) for it in items}
    low = {it: jnp.where(strict, b_col[it] * kk[it] * decay[it], 0.0) for it in items}
    attn = {it: jnp.where(tril, qk[it] * decay[it], 0.0).astype(BF16) for it in items}

    inv = {it: eye - jnp.where((ri >> 1) == (ci >> 1), low[it], 0.0) for it in items}
    size = 4
    while size <= c:
        bits = size.bit_length() - 1
        off = ((ri >> bits) == (ci >> bits)) & ((ri >> (bits - 1)) != (ci >> (bits - 1)))
        ib = {it: inv[it].astype(BF16) for it in items}
        right = {it: _dot(jnp.where(off, low[it], 0.0).astype(BF16), ib[it]).astype(BF16)
                 for it in items}
        inv = {it: inv[it] - _dot(ib[it], right[it]) for it in items}
        size *= 2

    tb = {it: inv[it] * b_row[it] for it in items}
    u = {it: _dot(tb[it].astype(BF16), vb[it]) for it in items}
    w = {it: _dot((tb[it] * jnp.exp(g_row[it])).astype(BF16), kb[it]).astype(BF16) for it in items}
    e_col = {it: jnp.exp(g_col[it]) for it in items}
    e_dec = {it: jnp.exp(g_last[it] - g_col[it]) for it in items}
    e_last = {it: jnp.exp(g_last[it]) for it in items}

    state = [state_ref[hh] for hh in heads]
    for ch in range(n_ch):
        sb = [state[hh].astype(BF16) for hh in heads]
        w_s = [_dot(w[(hh, ch)], sb[hh]) for hh in heads]
        q_s = [_dot(qb[(hh, ch)], sb[hh]) for hh in heads]
        v_new = [u[(hh, ch)] - w_s[hh] for hh in heads]
        v_dec = [(v_new[hh] * e_dec[(hh, ch)]).astype(BF16) for hh in heads]
        upd = [_dot_tn(kb[(hh, ch)], v_dec[hh]) for hh in heads]
        intra = [_dot(attn[(hh, ch)], v_new[hh].astype(BF16)) for hh in heads]
        for hh in heads:
            oacc[rows(ch), hh * hd:(hh + 1) * hd] = e_col[(hh, ch)] * q_s[hh] + intra[hh]
        state = [state[hh] * e_last[(hh, ch)] + upd[hh] for hh in heads]
    for hh in heads:
        state_ref[hh] = state[hh]

    z = z_ref[...]
    ng = ng_ref[...]
    for hh in range(GDN_GROUP):
        ls = slice(hh * hd, (hh + 1) * hd)
        o_ref[:, ls] = ((_rms(oacc[:, ls]) * ng) * _silu(z[:, ls])).astype(o_ref.dtype)


def _gdn(p_gdn, p_ab, conv_w, alog2, dtb2, norm_g, *, batch, seq, tt=1024):
    m = p_gdn.shape[0]
    gw = GDN_GROUP * GDN_HEAD_DIM
    n_groups = GDN_HEADS // GDN_GROUP
    nt = seq // tt

    def col(off):
        return pl.BlockSpec((tt, gw), lambda b, hg, t: (b * nt + t, off + hg))

    def cw(off):
        return pl.BlockSpec((GDN_CONV, gw), lambda b, hg, t: (0, off + hg))

    lane_spec = pl.BlockSpec((2 * GDN_GROUP, 1), lambda b, hg, t: (hg, 0))
    return pl.pallas_call(
        functools.partial(_gdn_kernel, tt=tt),
        out_shape=jax.ShapeDtypeStruct((m, GDN_HEADS * GDN_HEAD_DIM), BF16),
        grid=(batch, n_groups, nt),
        in_specs=[col(0), col(n_groups), col(2 * n_groups), col(3 * n_groups),
                  pl.BlockSpec((tt, LANES), lambda b, hg, t: (b * nt + t, hg)),
                  cw(0), cw(n_groups), cw(2 * n_groups),
                  lane_spec, lane_spec,
                  pl.BlockSpec((1, GDN_HEAD_DIM), lambda b, hg, t: (0, 0))],
        out_specs=pl.BlockSpec((tt, gw), lambda b, hg, t: (b * nt + t, hg)),
        scratch_shapes=[pltpu.VMEM((tt + GDN_HALO, gw), F32),
                        pltpu.VMEM((tt + GDN_HALO, gw), F32),
                        pltpu.VMEM((tt + GDN_HALO, gw), F32),
                        pltpu.VMEM((GDN_GROUP, GDN_HEAD_DIM, GDN_HEAD_DIM), F32),
                        pltpu.VMEM((tt, gw), F32)],
        compiler_params=_params("parallel", "parallel", "arbitrary"),
        name="gated_deltanet",
    )(p_gdn, p_gdn, p_gdn, p_gdn, p_ab, conv_w, conv_w, conv_w, alog2, dtb2,
      norm_g.reshape(1, GDN_HEAD_DIM))


MOBA_GROUP = 2


def _moba_kernel(q_ref, k_ref, v_ref, o_ref, kmean_ref, *, nb):
    blk = MOBA_BLOCK
    hd = MOBA_HEAD_DIM
    exp2_scale = (hd ** -0.5) * 1.4426950408889634

    def lanes(hh):
        return slice(hh * hd, (hh + 1) * hd)

    for hh in range(MOBA_GROUP):
        for n in range(nb):
            kblk = k_ref[n * blk:(n + 1) * blk, lanes(hh)].astype(F32)
            kmean_ref[hh * nb + n:hh * nb + n + 1, :] = jnp.sum(kblk, axis=0, keepdims=True) * (1.0 / blk)

    ri = lax.broadcasted_iota(jnp.int32, (blk, blk), 0)
    ci = lax.broadcasted_iota(jnp.int32, (blk, blk), 1)
    causal = ci <= ri

    def query(own, hh):
        return q_ref[own * blk:(own + 1) * blk, lanes(hh)]

    def scores(own, hh):
        return _dot_nt(query(own, hh), k_ref[0:(own + 1) * blk, lanes(hh)])

    def selection(own, hh):
        if own <= MOBA_TOPK:
            return None
        kmean = kmean_ref[hh * nb:(hh + 1) * nb, :]
        km_hi = kmean.astype(BF16)
        km_lo = (kmean - km_hi.astype(F32)).astype(BF16)
        gate = _dot_nt(km_hi, query(own, hh)) + _dot_nt(km_lo, query(own, hh))
        row = lax.broadcasted_iota(jnp.int32, (nb, blk), 0)
        rank = jnp.zeros((nb, blk), jnp.int32)
        for mth in range(own):
            gm = gate[mth:mth + 1, :]
            beats = (gm > gate) | ((gm == gate) & (row > mth))
            rank = rank + jnp.where(beats, 1, 0)
        chosen = jnp.where(rank < MOBA_TOPK, 1.0, 0.0)
        chosen = jnp.concatenate([chosen, jnp.zeros((LANES - nb, blk), F32)], axis=0)
        return chosen.T > 0.5

    items = [(own, hh) for own in range(nb) for hh in range(MOBA_GROUP)]
    s_next, sel_next = scores(*items[0]), selection(*items[0])
    for idx, (own, hh) in enumerate(items):
        s, sel = s_next, sel_next
        if idx + 1 < len(items):
            s_next, sel_next = scores(*items[idx + 1]), selection(*items[idx + 1])
        past = [s[:, n * blk:(n + 1) * blk] for n in range(own)]
        if sel is not None:
            past = [jnp.where(sel[:, n:n + 1], past[n], MASKED) for n in range(own)]
        pieces = past + [jnp.where(causal, s[:, own * blk:], MASKED)]
        m_run = jnp.max(functools.reduce(jnp.maximum, pieces), axis=-1, keepdims=True)
        probs = [jnp.exp2((pc - m_run) * exp2_scale) for pc in pieces]
        denom = jnp.sum(functools.reduce(lambda a, b: a + b, probs), axis=-1, keepdims=True)
        acc = _dot(jnp.concatenate([pr.astype(BF16) for pr in probs], axis=1),
                   v_ref[0:(own + 1) * blk, lanes(hh)])
        o_ref[own * blk:(own + 1) * blk, lanes(hh)] = (acc / denom).astype(o_ref.dtype)


def _moba(p_qk, p_v, *, batch, seq):
    m = p_qk.shape[0]
    nb = seq // MOBA_BLOCK
    gw = MOBA_GROUP * MOBA_HEAD_DIM
    n_groups = MOBA_HEADS // MOBA_GROUP
    return pl.pallas_call(
        functools.partial(_moba_kernel, nb=nb),
        out_shape=jax.ShapeDtypeStruct((m, MOBA_HEADS * MOBA_HEAD_DIM), BF16),
        grid=(batch, n_groups),
        in_specs=[pl.BlockSpec((seq, gw), lambda b, g: (b, g)),
                  pl.BlockSpec((seq, gw), lambda b, g: (b, n_groups + g)),
                  pl.BlockSpec((seq, gw), lambda b, g: (b, g))],
        out_specs=pl.BlockSpec((seq, gw), lambda b, g: (b, g)),
        scratch_shapes=[pltpu.VMEM((MOBA_GROUP * nb, MOBA_HEAD_DIM), F32)],
        compiler_params=_params("parallel", "parallel"),
        name="moba_attention",
    )(p_qk, p_qk, p_v)


def _out_proj_kernel(x_ref, mod_ref, post_ref, next_ref, yg_ref, ym_ref, w_ref, o_ref, hn_ref, w16,
                     *, row):
    @pl.when(pl.program_id(0) == 0)
    def _():
        w16[...] = w_ref[...].astype(BF16)

    kg = yg_ref.shape[1]
    y = _dot(yg_ref[...], w16[0:kg, :]) + _dot(ym_ref[...], w16[kg:, :])
    ga = mod_ref[0, row + 2:row + 3, :]
    out = x_ref[...] + ga * (_rms(y) * post_ref[...])
    o_ref[...] = out
    hn_ref[...] = _mod_norm(out, mod_ref, next_ref, row + 3).astype(hn_ref.dtype)


def _out_proj(x2, mod, row, post_g, next_g, y_gdn, y_moba, w_out, *, seq, tm=512):
    m, d = x2.shape
    per_b = seq // tm
    kg, km = y_gdn.shape[1], y_moba.shape[1]
    tile = pl.BlockSpec((tm, d), lambda i: (i, 0))
    row_spec = pl.BlockSpec((1, d), lambda i: (0, 0))
    return pl.pallas_call(
        functools.partial(_out_proj_kernel, row=row),
        out_shape=(jax.ShapeDtypeStruct((m, d), F32), jax.ShapeDtypeStruct((m, d), BF16)),
        grid=(m // tm,),
        in_specs=[tile,
                  pl.BlockSpec((1, N_MOD, d), lambda i: (i // per_b, 0, 0)),
                  row_spec, row_spec,
                  pl.BlockSpec((tm, kg), lambda i: (i, 0)),
                  pl.BlockSpec((tm, km), lambda i: (i, 0)),
                  pl.BlockSpec((kg + km, d), lambda i: (0, 0), pipeline_mode=pl.Buffered(1))],
        out_specs=(tile, tile),
        scratch_shapes=[pltpu.VMEM((kg + km, d), BF16)],
        compiler_params=_params("arbitrary"),
        name="mixer_out_proj",
    )(x2, mod, post_g.reshape(1, d), next_g.reshape(1, d), y_gdn, y_moba, w_out)


def _group_rows(vec):
    groups = vec.reshape(GDN_HEADS // GDN_GROUP, GDN_GROUP)
    return jnp.pad(groups, ((0, 0), (0, GDN_GROUP))).reshape(-1, 1)


def _gate_rows(w_a, w_b):
    d = w_a.shape[1]
    parts = []
    for hg in range(GDN_HEADS // GDN_GROUP):
        hs = slice(hg * GDN_GROUP, (hg + 1) * GDN_GROUP)
        parts += [w_a[hs], w_b[hs], jnp.zeros((LANES - 2 * GDN_GROUP, d), w_a.dtype)]
    return jnp.concatenate(parts, axis=0)


def kernel(x, c, w_ada, b_ada, ffn1_pre_g, ffn1_post_g, ffn1_w_gate, ffn1_w_up, ffn1_w_down, mix_pre_g, mix_post_g, w_in, gdn_conv_w, gdn_a_log, gdn_dt_bias, gdn_norm_g, w_out, ffn2_pre_g, ffn2_post_g, ffn2_w_gate, ffn2_w_up, ffn2_w_down):
    batch, seq, d = x.shape
    depth = w_ada.shape[0]
    gdn_w = GDN_HEADS * GDN_HEAD_DIM
    x2 = x.reshape(batch * seq, d)
    for l in range(depth):
        mod = _adaln(c, w_ada[l], b_ada[l]).reshape(batch, N_MOD, d)

        h1 = _norm_mod(x2, mod, 0, ffn1_pre_g[l], seq=seq)
        x2, h2 = _ffn(x2, h1, mod, 0, ffn1_post_g[l], ffn1_w_gate[l], ffn1_w_up[l], ffn1_w_down[l],
                      seq=seq, next_g=mix_pre_g[l])

        w_t = jnp.swapaxes(w_in[l], 0, 1)
        o_a = 4 * gdn_w
        o_b = o_a + GDN_HEADS
        w_ab_t = _gate_rows(w_t[o_a:o_b], w_t[o_b:o_b + GDN_HEADS])
        p_gdn, p_qk, p_v, p_ab = _in_proj(h2, w_t, w_ab_t, seq=seq)
        y_gdn = _gdn(p_gdn, p_ab, gdn_conv_w[l], _group_rows(gdn_a_log[l]),
                     _group_rows(gdn_dt_bias[l]), gdn_norm_g[l], batch=batch, seq=seq)
        y_moba = _moba(p_qk, p_v, batch=batch, seq=seq)
        x2, h3 = _out_proj(x2, mod, 3, mix_post_g[l], ffn2_pre_g[l], y_gdn, y_moba, w_out[l], seq=seq)

        x2 = _ffn(x2, h3, mod, 6, ffn2_post_g[l], ffn2_w_gate[l], ffn2_w_up[l], ffn2_w_down[l],
                  seq=seq)
    return x2.reshape(batch, seq, d)
```
